```python
import jax, jax.numpy as jnp
from jax import lax
import numpy as np

D_MODEL = 1024
BATCH = 4
SEQ = 8192
DEPTH = 2
DEC_BATCH = 8
DEC_SEQ = 4096
PAST_LEN = 128

N_META = 16
GRID_W = 64
N_HEADS = 8
HEAD_DIM = 64
ATTN_DIM = N_HEADS * HEAD_DIM
CONV_DIM = D_MODEL - ATTN_DIM
IN_DIM = 3 * ATTN_DIM + 2 * CONV_DIM
WIN_ROWS = 8
WIN_COLS = 16
CONV_WIDTH = 31
POOL_WINDOWS = (2, 4, 8, 16)
N_POOL_GROUPS = len(POOL_WINDOWS)
POOL_GROUP_DIM = D_MODEL // N_POOL_GROUPS
D_FF = 2816
N_EXPERTS = 8
TOP_K = 2
D_FF_EXPERT = 3584
MOE_BLOCK = 256
N_EVEN = (DEPTH + 1) // 2
N_ODD = DEPTH // 2
EPS = 1e-6

kernel_name = "hybrid_natten_conformer_pool_moe_encoder"


def rms_norm(x, gain):
    xf = x.astype(jnp.float32)
    y = xf * lax.rsqrt(jnp.mean(xf * xf, axis=-1, keepdims=True) + EPS)
    return (y * gain.astype(jnp.float32)).astype(x.dtype)


def layer_norm(x, gain, bias):
    xf = x.astype(jnp.float32)
    mu = jnp.mean(xf, axis=-1, keepdims=True)
    var = jnp.mean(jnp.square(xf - mu), axis=-1, keepdims=True)
    y = (xf - mu) * lax.rsqrt(var + EPS)
    return (y * gain.astype(jnp.float32) + bias.astype(jnp.float32)).astype(x.dtype)


def neighbourhood_attention(q, k, v, rel_bias):
    bsz, seq_len, n_heads, head_dim = q.shape
    n_tok = seq_len - N_META
    rows = n_tok // GRID_W
    kr = min(WIN_ROWS, rows)
    scale = head_dim ** -0.5
    qm, km, vm = q[:, :N_META], k[:, :N_META], v[:, :N_META]

    def to_grid(a):
        return a[:, N_META:].reshape(bsz, rows, GRID_W, n_heads, head_dim)

    qt, kt, vt = to_grid(q), to_grid(k), to_grid(v)

    s_mm = jnp.einsum('bqhd,bmhd->bhqm', qm, km).astype(jnp.float32) * scale
    p_mm = jax.nn.softmax(s_mm, axis=-1).astype(v.dtype)
    o_meta = jnp.einsum('bhqm,bmhd->bqhd', p_mm, vm)

    cols = np.arange(GRID_W)
    col_start = np.clip(cols - WIN_COLS // 2, 0, GRID_W - WIN_COLS)
    col_idx = col_start[:, None] + np.arange(WIN_COLS)[None, :]
    dc = col_idx - cols[:, None] + (WIN_COLS - 1)

    def row_block(r):
        rs = jnp.clip(r - kr // 2, 0, rows - kr)
        k_win = lax.dynamic_slice_in_dim(kt, rs, kr, axis=1)[:, :, col_idx]
        v_win = lax.dynamic_slice_in_dim(vt, rs, kr, axis=1)[:, :, col_idx]
        q_r = lax.dynamic_index_in_dim(qt, r, axis=1, keepdims=False)
        dr = rs + jnp.arange(kr) - r + (WIN_ROWS - 1)
        bias = jnp.transpose(rel_bias[:, dr][:, :, dc], (0, 2, 1, 3))
        s_loc = (jnp.einsum('bchd,bicjhd->bhcij', q_r, k_win).astype(jnp.float32) * scale
                 + bias.astype(jnp.float32)[None])
        s_meta = jnp.einsum('bchd,bmhd->bhcm', q_r, km).astype(jnp.float32) * scale
        logits = jnp.concatenate(
            [s_meta, s_loc.reshape(bsz, n_heads, GRID_W, kr * WIN_COLS)], axis=-1)
        p = jax.nn.softmax(logits, axis=-1).astype(v.dtype)
        p_meta = p[..., :N_META]
        p_loc = p[..., N_META:].reshape(bsz, n_heads, GRID_W, kr, WIN_COLS)
        return (jnp.einsum('bhcm,bmhd->bchd', p_meta, vm)
                + jnp.einsum('bhcij,bicjhd->bchd', p_loc, v_win))

    o_tok = lax.map(row_block, jnp.arange(rows))
    o_tok = jnp.moveaxis(o_tok, 0, 1).reshape(bsz, n_tok, n_heads, head_dim)
    return jnp.concatenate([o_meta, o_tok], axis=1)


def even_mixer(h, w_in, q_gain, k_gain, rel_bias, conv_w, conv_b, ln_g, ln_b, w_out):
    bsz, seq_len, _ = h.shape
    proj = h @ w_in
    q, k, v, u, g = jnp.split(
        proj, [ATTN_DIM, 2 * ATTN_DIM, 3 * ATTN_DIM, 3 * ATTN_DIM + CONV_DIM], axis=-1)

    def heads(a):
        return a.reshape(bsz, seq_len, N_HEADS, HEAD_DIM)

    q = rms_norm(heads(q), q_gain)
    k = rms_norm(heads(k), k_gain)
    attn = neighbourhood_attention(q, k, heads(v), rel_bias).reshape(bsz, seq_len, ATTN_DIM)

    c = u * jax.nn.sigmoid(g)
    c = lax.conv_general_dilated(
        c, conv_w[:, None, :], window_strides=(1,),
        padding=[(CONV_WIDTH // 2, CONV_WIDTH // 2)],
        dimension_numbers=('NWC', 'WIO', 'NWC'),
        feature_group_count=CONV_DIM) + conv_b
    c = jax.nn.silu(layer_norm(c, ln_g, ln_b))
    return jnp.concatenate([attn, c], axis=-1) @ w_out


def pool_mixer(h, w_pool, pool_scale):
    bsz, seq_len, _ = h.shape
    hf = h.astype(jnp.float32)
    csum = jnp.concatenate(
        [jnp.zeros((bsz, 1, D_MODEL), jnp.float32), jnp.cumsum(hf, axis=1)], axis=1)
    t = np.arange(seq_len)
    diffs = []
    for gi, w in enumerate(POOL_WINDOWS):
        sl = slice(gi * POOL_GROUP_DIM, (gi + 1) * POOL_GROUP_DIM)
        lo = np.clip(t - w // 2, 0, seq_len)
        hi = np.clip(t - w // 2 + w, 0, seq_len)
        count = (hi - lo).astype(np.float32)[None, :, None]
        mean = (csum[:, hi, sl] - csum[:, lo, sl]) / count
        diffs.append(mean - hf[:, :, sl])
    d = jnp.stack(diffs, axis=2).astype(h.dtype)
    mixed = jnp.einsum('blgc,gce->blge', d, w_pool).reshape(bsz, seq_len, D_MODEL)
    return mixed * pool_scale


def swiglu(h, w1, w3, w2):
    return (jax.nn.silu(h @ w1) * (h @ w3)) @ w2


def moe_swiglu(h, router_w, w1, w3, w2):
    n_tok = h.shape[0]
    logits = h.astype(jnp.float32) @ router_w.astype(jnp.float32)
    top_vals, top_idx = lax.top_k(logits, TOP_K)
    gates = jax.nn.softmax(top_vals, axis=-1)
    n_assign = n_tok * TOP_K
    e_flat = top_idx.reshape(-1)
    tok_flat = jnp.repeat(jnp.arange(n_tok, dtype=jnp.int32), TOP_K)
    g_flat = gates.reshape(-1)
    order = jnp.argsort(e_flat)
    e_s, tok_s, g_s = e_flat[order], tok_flat[order], g_flat[order]
    counts = jnp.bincount(e_flat, length=N_EXPERTS)
    padded = (counts + MOE_BLOCK - 1) // MOE_BLOCK * MOE_BLOCK
    pad_end = jnp.cumsum(padded)
    pad_start = pad_end - padded
    raw_start = jnp.cumsum(counts) - counts
    dest = pad_start[e_s] + jnp.arange(n_assign, dtype=jnp.int32) - raw_start[e_s]
    n_blocks = n_assign // MOE_BLOCK + N_EXPERTS
    n_slots = n_blocks * MOE_BLOCK
    slot_tok = jnp.full((n_slots,), n_tok, jnp.int32).at[dest].set(tok_s)
    slot_gate = jnp.zeros((n_slots,), jnp.float32).at[dest].set(g_s)
    block_e = jnp.clip(
        jnp.searchsorted(pad_end, jnp.arange(n_blocks, dtype=jnp.int32) * MOE_BLOCK, side='right'),
        0, N_EXPERTS - 1)
    h_pad = jnp.concatenate([h, jnp.zeros((1, D_MODEL), h.dtype)], axis=0)
    xb = h_pad[slot_tok].reshape(n_blocks, MOE_BLOCK, D_MODEL)

    def expert_block(args):
        x_blk, e = args
        return swiglu(x_blk, w1[e], w3[e], w2[e])

    yb = lax.map(expert_block, (xb, block_e)).reshape(n_slots, D_MODEL)
    out = jnp.zeros((n_tok + 1, D_MODEL), h.dtype).at[slot_tok].add(
        yb * slot_gate[:, None].astype(yb.dtype))
    return out[:n_tok]


def trunk(x, meta_tokens, norm_mix_e, w_in, q_gain, k_gain, rel_bias, conv_w, conv_b,
          conv_ln_g, conv_ln_b, w_out, norm_ffn_e, ffn_w1, ffn_w3, ffn_w2,
          norm_mix_o, pool_w, pool_scale, norm_ffn_o, router_w, moe_w1, moe_w3, moe_w2):
    bsz = x.shape[0]
    meta = jnp.broadcast_to(meta_tokens.astype(x.dtype)[None], (bsz, N_META, D_MODEL))
    h = jnp.concatenate([meta, x], axis=1)
    for layer in range(DEPTH):
        if layer % 2 == 0:
            i = layer // 2
            h = h + even_mixer(rms_norm(h, norm_mix_e[i]), w_in[i], q_gain[i], k_gain[i],
                               rel_bias[i], conv_w[i], conv_b[i], conv_ln_g[i], conv_ln_b[i],
                               w_out[i])
            h = h + swiglu(rms_norm(h, norm_ffn_e[i]), ffn_w1[i], ffn_w3[i], ffn_w2[i])
        else:
            j = layer // 2
            h = h + pool_mixer(rms_norm(h, norm_mix_o[j]), pool_w[j], pool_scale[j])
            hn = rms_norm(h, norm_ffn_o[j]).reshape(-1, D_MODEL)
            h = h + moe_swiglu(hn, router_w[j], moe_w1[j], moe_w3[j], moe_w2[j]).reshape(h.shape)
    return h[:, N_META:]


def setup_inputs(seed: int = 0) -> dict:
    key = jax.random.key(seed)
    ks = jax.random.split(key, 26)
    f32 = jnp.float32

    def nrm(k, shape, scale):
        return jax.random.normal(k, shape, f32) * scale

    def gain(k, shape):
        return 1.0 + 0.02 * jax.random.normal(k, shape, f32)

    return {
        "x_prompt": nrm(ks[0], (BATCH, SEQ, D_MODEL), 1.0),
        "x_sample": nrm(ks[1], (DEC_BATCH, DEC_SEQ, D_MODEL), 1.0),
        "meta_tokens": nrm(ks[2], (N_META, D_MODEL), 1.0),
        "norm_mix_e": gain(ks[3], (N_EVEN, D_MODEL)),
        "w_in": nrm(ks[4], (N_EVEN, D_MODEL, IN_DIM), D_MODEL ** -0.5),
        "q_gain": gain(ks[5], (N_EVEN, HEAD_DIM)),
        "k_gain": gain(ks[6], (N_EVEN, HEAD_DIM)),
        "rel_bias": nrm(ks[7], (N_EVEN, N_HEADS, 2 * WIN_ROWS - 1, 2 * WIN_COLS - 1), 0.1),
        "conv_w": nrm(ks[8], (N_EVEN, CONV_WIDTH, CONV_DIM), CONV_WIDTH ** -0.5),
        "conv_b": nrm(ks[9], (N_EVEN, CONV_DIM), 0.01),
        "conv_ln_g": gain(ks[10], (N_EVEN, CONV_DIM)),
        "conv_ln_b": nrm(ks[11], (N_EVEN, CONV_DIM), 0.01),
        "w_out": nrm(ks[12], (N_EVEN, D_MODEL, D_MODEL), D_MODEL ** -0.5),
        "norm_ffn_e": gain(ks[13], (N_EVEN, D_MODEL)),
        "ffn_w1": nrm(ks[14], (N_EVEN, D_MODEL, D_FF), D_MODEL ** -0.5),
        "ffn_w3": nrm(ks[15], (N_EVEN, D_MODEL, D_FF), D_MODEL ** -0.5),
        "ffn_w2": nrm(ks[16], (N_EVEN, D_FF, D_MODEL), D_FF ** -0.5),
        "norm_mix_o": gain(ks[17], (N_ODD, D_MODEL)),
        "pool_w": nrm(ks[18], (N_ODD, N_POOL_GROUPS, POOL_GROUP_DIM, POOL_GROUP_DIM), POOL_GROUP_DIM ** -0.5),
        "pool_scale": gain(ks[19], (N_ODD, D_MODEL)),
        "norm_ffn_o": gain(ks[20], (N_ODD, D_MODEL)),
        "router_w": nrm(ks[21], (N_ODD, D_MODEL, N_EXPERTS), D_MODEL ** -0.5),
        "moe_w1": nrm(ks[22], (N_ODD, N_EXPERTS, D_MODEL, D_FF_EXPERT), D_MODEL ** -0.5),
        "moe_w3": nrm(ks[23], (N_ODD, N_EXPERTS, D_MODEL, D_FF_EXPERT), D_MODEL ** -0.5),
        "moe_w2": nrm(ks[24], (N_ODD, N_EXPERTS, D_FF_EXPERT, D_MODEL), D_FF_EXPERT ** -0.5),
    }


def reference(x_prompt, x_sample, meta_tokens, norm_mix_e, w_in, q_gain, k_gain, rel_bias,
              conv_w, conv_b, conv_ln_g, conv_ln_b, w_out, norm_ffn_e, ffn_w1, ffn_w3, ffn_w2,
              norm_mix_o, pool_w, pool_scale, norm_ffn_o, router_w, moe_w1, moe_w3, moe_w2):
    params = (meta_tokens, norm_mix_e, w_in, q_gain, k_gain, rel_bias, conv_w, conv_b,
              conv_ln_g, conv_ln_b, w_out, norm_ffn_e, ffn_w1, ffn_w3, ffn_w2,
              norm_mix_o, pool_w, pool_scale, norm_ffn_o, router_w, moe_w1, moe_w3, moe_w2)
    y_prompt = trunk(x_prompt, *params)
    y_sample = trunk(x_sample, *params)
    return (y_prompt, y_sample)
```

```python
import functools

import numpy as np
import jax
import jax.numpy as jnp
from jax import lax
from jax.experimental import pallas as pl
from jax.experimental.pallas import tpu as pltpu

D_MODEL = 1024
N_META = 16
GRID_W = 64
N_HEADS = 8
HEAD_DIM = 64
ATTN_DIM = N_HEADS * HEAD_DIM
CONV_DIM = D_MODEL - ATTN_DIM
IN_DIM = 3 * ATTN_DIM + 2 * CONV_DIM
WIN_ROWS = 8
WIN_COLS = 16
CONV_WIDTH = 31
CONV_HALF = CONV_WIDTH // 2
POOL_WINDOWS = (2, 4, 8, 16)
POOL_GROUP_DIM = D_MODEL // len(POOL_WINDOWS)
D_FF = 2816
N_EXPERTS = 8
TOP_K = 2
D_FF_EXPERT = 3584
EPS = 1e-6

F32 = jnp.float32
BF16 = jnp.bfloat16

TOKEN_TILE = 512
ATTN_ROWS = 4
ATTN_Q = ATTN_ROWS * GRID_W
ATTN_KROWS = ATTN_ROWS + WIN_ROWS
ATTN_KBLK = ATTN_KROWS * GRID_W // ATTN_Q
CONV_HALO = 16
CONV_CHUNK = 64
POOL_HALO = 8
EXPERT_TILE = 1024
EXPERT_FF_CHUNK = 512
DMA_TILE = 256
NEG_INF = -1e30
VMEM_LIMIT = 56 * 1024 * 1024


def _params(sem):
    return pltpu.CompilerParams(dimension_semantics=sem, vmem_limit_bytes=VMEM_LIMIT)


def _const_spec(shape):
    zeros = (0,) * len(shape)
    return pl.BlockSpec(shape, lambda *_: zeros)


def _row_spec(tm, cols):
    return pl.BlockSpec((tm, cols), lambda i: (i, 0))


def _rms(x, gain):
    ms = jnp.mean(x * x, axis=-1, keepdims=True)
    return x * lax.rsqrt(ms + EPS) * gain


def _proj_kernel(x_ref, g_ref, w_ref, qg_ref, kg_ref, hsum_ref, q_ref, k_ref, v_ref, c_ref):
    n = _rms(x_ref[...], g_ref[...]).astype(BF16)
    proj = jnp.dot(n, w_ref[...], preferred_element_type=F32)
    q = proj[:, :ATTN_DIM]
    k = proj[:, ATTN_DIM:2 * ATTN_DIM]
    v = proj[:, 2 * ATTN_DIM:3 * ATTN_DIM]
    u = proj[:, 3 * ATTN_DIM:3 * ATTN_DIM + CONV_DIM]
    g = proj[:, 3 * ATTN_DIM + CONV_DIM:]

    def head_norm(a, gain):
        ss = jnp.dot((a * a).astype(BF16), hsum_ref[...], preferred_element_type=F32)
        return a * lax.rsqrt(ss * (1.0 / HEAD_DIM) + EPS) * gain

    q_ref[...] = (head_norm(q, qg_ref[...]) * (HEAD_DIM ** -0.5)).astype(BF16)
    k_ref[...] = head_norm(k, kg_ref[...]).astype(BF16)
    v_ref[...] = v.astype(BF16)
    c_ref[...] = (u * jax.nn.sigmoid(g)).astype(BF16)


def _proj(x, norm_g, w_in, qg, kg, hsum, tm):
    n = x.shape[0]
    out = jax.ShapeDtypeStruct((n, ATTN_DIM), BF16)
    return pl.pallas_call(
        _proj_kernel,
        grid=(n // tm,),
        in_specs=[_row_spec(tm, D_MODEL), _const_spec((1, D_MODEL)), _const_spec((D_MODEL, IN_DIM)),
                  _const_spec((1, ATTN_DIM)), _const_spec((1, ATTN_DIM)), _const_spec((ATTN_DIM, ATTN_DIM))],
        out_specs=[_row_spec(tm, ATTN_DIM)] * 4,
        out_shape=[out] * 4,
        compiler_params=_params(("arbitrary",)),
        name="in_proj",
    )(x, norm_g, w_in, qg, kg, hsum)


def _attn_bias_patterns(rel_bias):
    qr = np.arange(ATTN_ROWS)[:, None, None, None]
    qc = np.arange(GRID_W)[None, :, None, None]
    kr = np.arange(ATTN_KROWS)[None, None, :, None]
    kc = np.arange(GRID_W)[None, None, None, :]
    cs = np.clip(qc - WIN_COLS // 2, 0, GRID_W - WIN_COLS)
    col_ok = (kc >= cs) & (kc < cs + WIN_COLS)
    dc = np.clip(kc - qc + (WIN_COLS - 1), 0, 2 * WIN_COLS - 2)
    pats = []
    for off, rs in ((0, 0 * qr), (ATTN_ROWS, qr), (2 * ATTN_ROWS, ATTN_ROWS + 0 * qr)):
        row_ok = (kr >= rs) & (kr < rs + WIN_ROWS)
        dr = np.clip(kr - (off + qr) + (WIN_ROWS - 1), 0, 2 * WIN_ROWS - 2)
        ok = np.broadcast_to(row_ok & col_ok, (ATTN_ROWS, GRID_W, ATTN_KROWS, GRID_W))
        dr_b = np.broadcast_to(dr, ok.shape).reshape(ATTN_Q, ATTN_KROWS * GRID_W)
        dc_b = np.broadcast_to(dc, ok.shape).reshape(ATTN_Q, ATTN_KROWS * GRID_W)
        vals = rel_bias.astype(F32)[:, dr_b, dc_b]
        pats.append(jnp.where(ok.reshape(1, ATTN_Q, -1), vals, NEG_INF))
    return jnp.stack(pats, axis=0)


def _attn_kernel(q_ref, k0_ref, k1_ref, k2_ref, v0_ref, v1_ref, v2_ref, km_ref, vm_ref, bias_ref, o_ref):
    nt = (((1,), (1,)), ((), ()))
    k_refs = (k0_ref, k1_ref, k2_ref)
    v_refs = (v0_ref, v1_ref, v2_ref)
    outs = []
    for h in range(N_HEADS):
        sl = slice(h * HEAD_DIM, (h + 1) * HEAD_DIM)
        qh = q_ref[:, sl]
        s_meta = lax.dot_general(qh, km_ref[:, sl], nt, preferred_element_type=F32)
        s_loc = [lax.dot_general(qh, k_refs[t][:, sl], nt, preferred_element_type=F32)
                 + bias_ref[0, h, :, t * ATTN_Q:(t + 1) * ATTN_Q] for t in range(ATTN_KBLK)]
        m = jnp.max(s_meta, axis=-1, keepdims=True)
        for s in s_loc:
            m = jnp.maximum(m, jnp.max(s, axis=-1, keepdims=True))
        p_meta = jnp.exp(s_meta - m)
        denom = jnp.sum(p_meta, axis=-1, keepdims=True)
        acc = jnp.dot(p_meta.astype(BF16), vm_ref[:, sl], preferred_element_type=F32)
        for t in range(ATTN_KBLK):
            p = jnp.exp(s_loc[t] - m)
            denom = denom + jnp.sum(p, axis=-1, keepdims=True)
            acc = acc + jnp.dot(p.astype(BF16), v_refs[t][:, sl], preferred_element_type=F32)
        outs.append(acc / denom)
    o_ref[...] = jnp.concatenate(outs, axis=-1).astype(BF16)


def _attention(q, k, v, km, vm, bias_pat, bsz, seq):
    nb = seq // ATTN_Q
    assert seq % ATTN_Q == 0 and nb >= ATTN_KBLK

    def q_map(b, j):
        return (b * nb + j, 0)

    def kv_map(t):
        return lambda b, j: (b * nb + jnp.clip(j - 1, 0, nb - ATTN_KBLK) + t, 0)

    def bias_map(b, j):
        return (jnp.where(j == 0, 0, jnp.where(j == nb - 1, 2, 1)), 0, 0, 0)

    blk = (ATTN_Q, ATTN_DIM)
    kv_specs = [pl.BlockSpec(blk, kv_map(t)) for t in range(ATTN_KBLK)]
    return pl.pallas_call(
        _attn_kernel,
        grid=(bsz, nb),
        in_specs=[pl.BlockSpec(blk, q_map)] + kv_specs + kv_specs
                 + [_const_spec((N_META, ATTN_DIM)), _const_spec((N_META, ATTN_DIM)),
                    pl.BlockSpec((1, N_HEADS, ATTN_Q, ATTN_KBLK * ATTN_Q), bias_map)],
        out_specs=pl.BlockSpec(blk, q_map),
        out_shape=jax.ShapeDtypeStruct((bsz * seq, ATTN_DIM), BF16),
        compiler_params=_params(("arbitrary", "arbitrary")),
        name="nbr_attn",
    )(q, k, k, k, v, v, v, km, vm, bias_pat)


def _meta_attn_kernel(q_ref, k_ref, v_ref, o_ref):
    nt = (((1,), (1,)), ((), ()))
    outs = []
    for h in range(N_HEADS):
        sl = slice(h * HEAD_DIM, (h + 1) * HEAD_DIM)
        s = lax.dot_general(q_ref[:, sl], k_ref[:, sl], nt, preferred_element_type=F32)
        p = jnp.exp(s - jnp.max(s, axis=-1, keepdims=True))
        o = jnp.dot(p.astype(BF16), v_ref[:, sl], preferred_element_type=F32)
        outs.append(o / jnp.sum(p, axis=-1, keepdims=True))
    o_ref[...] = jnp.concatenate(outs, axis=-1).astype(BF16)


def _meta_attention(qm, km, vm):
    spec = _const_spec((N_META, ATTN_DIM))
    return pl.pallas_call(
        _meta_attn_kernel,
        in_specs=[spec] * 3, out_specs=spec, grid=(1,),
        out_shape=jax.ShapeDtypeStruct((N_META, ATTN_DIM), BF16),
        compiler_params=_params(("arbitrary",)),
        name="meta_attn",
    )(qm, km, vm)


def _conv_kernel(c_ref, prev_ref, next_ref, cm_ref, w_ref, b_ref, lg_ref, lb_ref,
                 out_ref, outm_ref, xpad_ref, mpad_ref):
    tile = c_ref.shape[0]
    i = pl.program_id(1)
    first = i == 0
    last = i == pl.num_programs(1) - 1
    cm = cm_ref[...].astype(F32)
    xpad_ref[0:CONV_HALO, :] = jnp.where(first, cm, prev_ref[...].astype(F32))
    xpad_ref[CONV_HALO:CONV_HALO + tile, :] = c_ref[...].astype(F32)
    xpad_ref[CONV_HALO + tile:, :] = jnp.where(last, 0.0, next_ref[...].astype(F32))

    def conv_rows(pad_ref, start, rows):
        acc = jnp.zeros((rows, CONV_DIM), F32)
        for j in range(CONV_WIDTH):
            lo = start - CONV_HALF + j
            acc = acc + w_ref[j:j + 1, :] * pad_ref[lo:lo + rows, :]
        y = acc + b_ref[...]
        mu = jnp.mean(y, axis=-1, keepdims=True)
        yc = y - mu
        var = jnp.mean(yc * yc, axis=-1, keepdims=True)
        z = yc * lax.rsqrt(var + EPS) * lg_ref[...] + lb_ref[...]
        return (z * jax.nn.sigmoid(z)).astype(BF16)

    for rc in range(tile // CONV_CHUNK):
        out_ref[rc * CONV_CHUNK:(rc + 1) * CONV_CHUNK, :] = conv_rows(
            xpad_ref, CONV_HALO + rc * CONV_CHUNK, CONV_CHUNK)

    @pl.when(first)
    def _():
        mpad_ref[0:N_META, :] = jnp.zeros((N_META, CONV_DIM), F32)
        mpad_ref[N_META:2 * N_META, :] = cm
        mpad_ref[2 * N_META:, :] = c_ref[0:N_META, :].astype(F32)
        outm_ref[...] = conv_rows(mpad_ref, N_META, N_META)


def _conv(c, cm, conv_w, conv_b, ln_g, ln_b, bsz, seq, tile):
    tps = seq // tile
    hpt = tile // CONV_HALO
    n_halo = bsz * seq // CONV_HALO

    def main_map(b, i):
        return (b * tps + i, 0)

    def prev_map(b, i):
        return (jnp.maximum((b * tps + i) * hpt - 1, 0), 0)

    def next_map(b, i):
        return (jnp.minimum((b * tps + i + 1) * hpt, n_halo - 1), 0)

    halo = (CONV_HALO, CONV_DIM)
    vec = _const_spec((1, CONV_DIM))
    return pl.pallas_call(
        _conv_kernel,
        grid=(bsz, tps),
        in_specs=[pl.BlockSpec((tile, CONV_DIM), main_map), pl.BlockSpec(halo, prev_map),
                  pl.BlockSpec(halo, next_map), _const_spec((N_META, CONV_DIM)),
                  _const_spec((CONV_WIDTH + 1, CONV_DIM)), vec, vec, vec],
        out_specs=[pl.BlockSpec((tile, CONV_DIM), main_map),
                   pl.BlockSpec((N_META, CONV_DIM), lambda b, i: (b, 0))],
        out_shape=[jax.ShapeDtypeStruct((bsz * seq, CONV_DIM), BF16),
                   jax.ShapeDtypeStruct((bsz * N_META, CONV_DIM), BF16)],
        scratch_shapes=[pltpu.VMEM((tile + 2 * CONV_HALO, CONV_DIM), F32),
                        pltpu.VMEM((3 * N_META, CONV_DIM), F32)],
        compiler_params=_params(("arbitrary", "arbitrary")),
        name="conv_ln_silu",
    )(c, c, c, cm, conv_w, conv_b, ln_g, ln_b)


def _out_proj_kernel(x_ref, a_ref, c_ref, wa_ref, wc_ref, o_ref):
    o_ref[...] = (x_ref[...]
                  + jnp.dot(a_ref[...], wa_ref[...], preferred_element_type=F32)
                  + jnp.dot(c_ref[...], wc_ref[...], preferred_element_type=F32))


def _out_proj(x, attn, cact, w_attn, w_conv, tm):
    n = x.shape[0]
    wspec = _const_spec((ATTN_DIM, D_MODEL))
    return pl.pallas_call(
        _out_proj_kernel,
        grid=(n // tm,),
        in_specs=[_row_spec(tm, D_MODEL), _row_spec(tm, ATTN_DIM), _row_spec(tm, CONV_DIM), wspec, wspec],
        out_specs=_row_spec(tm, D_MODEL),
        out_shape=jax.ShapeDtypeStruct((n, D_MODEL), F32),
        compiler_params=_params(("arbitrary",)),
        name="out_proj",
    )(x, attn, cact, w_attn, w_conv)


def _ffn_kernel(x_ref, g_ref, w1_ref, w3_ref, w2_ref, o_ref):
    x = x_ref[...]
    n = _rms(x, g_ref[...]).astype(BF16)
    h1 = jnp.dot(n, w1_ref[...], preferred_element_type=F32)
    h3 = jnp.dot(n, w3_ref[...], preferred_element_type=F32)
    act = (h1 * jax.nn.sigmoid(h1) * h3).astype(BF16)
    o_ref[...] = x + jnp.dot(act, w2_ref[...], preferred_element_type=F32)


def _ffn(x, g, w1, w3, w2, tm):
    n = x.shape[0]
    return pl.pallas_call(
        _ffn_kernel,
        grid=(n // tm,),
        in_specs=[_row_spec(tm, D_MODEL), _const_spec((1, D_MODEL)), _const_spec((D_MODEL, D_FF)),
                  _const_spec((D_MODEL, D_FF)), _const_spec((D_FF, D_MODEL))],
        out_specs=_row_spec(tm, D_MODEL),
        out_shape=jax.ShapeDtypeStruct((n, D_MODEL), F32),
        compiler_params=_params(("arbitrary",)),
        name="dense_swiglu",
    )(x, g, w1, w3, w2)


def _pool_router_kernel(x_ref, prev_ref, next_ref, xm_ref, g_ref, wp_ref, ps_ref, g2_ref, rhi_ref, rlo_ref,
                        x3_ref, hn_ref, idx_ref, gate_ref, npad_ref, *, seq):
    tile = x_ref.shape[0]
    i = pl.program_id(1)
    first = i == 0
    last = i == pl.num_programs(1) - 1
    gain = g_ref[...]
    x = x_ref[...]
    n_main = _rms(x, gain)
    npad_ref[0:POOL_HALO, :] = _rms(jnp.where(first, xm_ref[...], prev_ref[...]), gain)
    npad_ref[POOL_HALO:POOL_HALO + tile, :] = n_main
    npad_ref[POOL_HALO + tile:, :] = jnp.where(last, 0.0, _rms(next_ref[...], gain))

    tok = i * tile + lax.broadcasted_iota(jnp.int32, (tile, 1), 0)
    mixed = []
    for gi, w in enumerate(POOL_WINDOWS):
        cols = slice(gi * POOL_GROUP_DIM, (gi + 1) * POOL_GROUP_DIM)
        half = w // 2
        acc = jnp.zeros((tile, POOL_GROUP_DIM), F32)
        for d in range(-half, half):
            acc = acc + npad_ref[POOL_HALO + d:POOL_HALO + d + tile, cols]
        count = (w - jnp.maximum(tok + half - seq, 0)).astype(F32)
        diff = (acc / count - n_main[:, cols]).astype(BF16)
        mixed.append(jnp.dot(diff, wp_ref[gi], preferred_element_type=F32))
    x3 = x + jnp.concatenate(mixed, axis=-1) * ps_ref[...]
    x3_ref[...] = x3

    hn = _rms(x3, g2_ref[...])
    hn_ref[...] = hn
    h_hi = hn.astype(BF16)
    h_lo = (hn - h_hi.astype(F32)).astype(BF16)
    logits = (jnp.dot(h_hi, rhi_ref[...], preferred_element_type=F32)
              + jnp.dot(h_lo, rhi_ref[...], preferred_element_type=F32)
              + jnp.dot(h_hi, rlo_ref[...], preferred_element_type=F32))
    lane = lax.broadcasted_iota(jnp.int32, logits.shape, 1)
    lane_f = lane.astype(F32)
    logits = jnp.where(lane < N_EXPERTS, logits, NEG_INF)
    m1 = jnp.max(logits, axis=-1, keepdims=True)
    i1 = jnp.min(jnp.where(logits == m1, lane_f, 256.0), axis=-1, keepdims=True)
    rest = jnp.where(lane_f == i1, NEG_INF, logits)
    m2 = jnp.max(rest, axis=-1, keepdims=True)
    i2 = jnp.min(jnp.where(rest == m2, lane_f, 256.0), axis=-1, keepdims=True)
    e2 = jnp.exp(m2 - m1)
    g1 = 1.0 / (1.0 + e2)
    idx_ref[...] = jnp.where(lane == 0, i1, jnp.where(lane == 1, i2, 0.0)).astype(jnp.int32)
    gate_ref[...] = jnp.where(lane == 0, g1, jnp.where(lane == 1, e2 * g1, 0.0))


def _pool_router(x2, x2m, g, w_pool, pool_scale, g2, r_hi, r_lo, bsz, seq, tile):
    tps = seq // tile
    hpt = tile // POOL_HALO
    n_halo = bsz * seq // POOL_HALO

    def main_map(b, i):
        return (b * tps + i, 0)

    def prev_map(b, i):
        return (jnp.maximum((b * tps + i) * hpt - 1, 0), 0)

    def next_map(b, i):
        return (jnp.minimum((b * tps + i + 1) * hpt, n_halo - 1), 0)

    halo = (POOL_HALO, D_MODEL)
    vec = _const_spec((1, D_MODEL))
    main = pl.BlockSpec((tile, D_MODEL), main_map)
    small = pl.BlockSpec((tile, 128), main_map)
    n = bsz * seq
    return pl.pallas_call(
        functools.partial(_pool_router_kernel, seq=seq),
        grid=(bsz, tps),
        in_specs=[main, pl.BlockSpec(halo, prev_map), pl.BlockSpec(halo, next_map),
                  pl.BlockSpec(halo, lambda b, i: (2 * b + 1, 0)), vec,
                  _const_spec((len(POOL_WINDOWS), POOL_GROUP_DIM, POOL_GROUP_DIM)), vec, vec,
                  _const_spec((D_MODEL, 128)), _const_spec((D_MODEL, 128))],
        out_specs=[main, main, small, small],
        out_shape=[jax.ShapeDtypeStruct((n, D_MODEL), F32), jax.ShapeDtypeStruct((n, D_MODEL), F32),
                   jax.ShapeDtypeStruct((n, 128), jnp.int32), jax.ShapeDtypeStruct((n, 128), F32)],
        scratch_shapes=[pltpu.VMEM((tile + 2 * POOL_HALO, D_MODEL), F32)],
        compiler_params=_params(("arbitrary", "arbitrary")),
        name="pool_router",
    )(x2, x2, x2, x2m, g, w_pool, pool_scale, g2, r_hi, r_lo)


def _dispatch_kernel(dest_ref, hn_ref, xs_in_ref, xs_ref, sem):
    del xs_in_ref
    tile = hn_ref.shape[0]

    def copy(r, k):
        return pltpu.make_async_copy(hn_ref.at[pl.ds(r, 1)], xs_ref.at[pl.ds(dest_ref[k, r], 1)], sem)

    def start(r, carry):
        for k in range(TOP_K):
            copy(r, k).start()
        return carry

    def wait(r, carry):
        for k in range(TOP_K):
            copy(r, k).wait()
        return carry

    lax.fori_loop(0, tile, start, 0)
    lax.fori_loop(0, tile, wait, 0)


def _dispatch(dest_tiles, hn, xs):
    n_tiles, _, tile = dest_tiles.shape
    return pl.pallas_call(
        _dispatch_kernel,
        grid=(n_tiles,),
        in_specs=[pl.BlockSpec((None, TOP_K, tile), lambda i: (i, 0, 0), memory_space=pltpu.SMEM),
                  _row_spec(tile, D_MODEL), pl.BlockSpec(memory_space=pl.ANY)],
        out_specs=pl.BlockSpec(memory_space=pl.ANY),
        out_shape=jax.ShapeDtypeStruct(xs.shape, xs.dtype),
        scratch_shapes=[pltpu.SemaphoreType.DMA],
        input_output_aliases={2: 0},
        compiler_params=_params(("arbitrary",)),
        name="moe_dispatch",
    )(dest_tiles, hn, xs)


def _expert_kernel(te_ref, tv_ref, x_ref, w1_ref, w3_ref, w2_ref, y_ref, xb_ref, acc_ref):
    del te_ref
    j = pl.program_id(0)
    c = pl.program_id(1)
    valid = tv_ref[j] == 1

    @pl.when(valid & (c == 0))
    def _():
        xb_ref[...] = x_ref[...].astype(BF16)

    @pl.when(valid)
    def _():
        xb = xb_ref[...]
        h1 = jnp.dot(xb, w1_ref[...], preferred_element_type=F32)
        h3 = jnp.dot(xb, w3_ref[...], preferred_element_type=F32)
        act = (h1 * jax.nn.sigmoid(h1) * h3).astype(BF16)
        part = jnp.dot(act, w2_ref[...], preferred_element_type=F32)

        @pl.when(c == 0)
        def _():
            acc_ref[...] = part

        @pl.when(c > 0)
        def _():
            acc_ref[...] += part

    last = c == pl.num_programs(1) - 1

    @pl.when(valid & last)
    def _():
        y_ref[...] = acc_ref[...]

    @pl.when(jnp.logical_not(valid) & last)
    def _():
        y_ref[...] = jnp.zeros(y_ref.shape, F32)


def _experts(tile_expert, tile_valid, xs, w1, w3, w2):
    n_slots = xs.shape[0]
    n_tiles = n_slots // EXPERT_TILE
    n_chunks = D_FF_EXPERT // EXPERT_FF_CHUNK

    def chunk(j, c, tv):
        return jnp.where(tv[j] == 1, c, n_chunks - 1)

    grid_spec = pltpu.PrefetchScalarGridSpec(
        num_scalar_prefetch=2,
        grid=(n_tiles, n_chunks),
        in_specs=[pl.BlockSpec((EXPERT_TILE, D_MODEL), lambda j, c, te, tv: (j, 0)),
                  pl.BlockSpec((None, D_MODEL, EXPERT_FF_CHUNK), lambda j, c, te, tv: (te[j], 0, chunk(j, c, tv))),
                  pl.BlockSpec((None, D_MODEL, EXPERT_FF_CHUNK), lambda j, c, te, tv: (te[j], 0, chunk(j, c, tv))),
                  pl.BlockSpec((None, EXPERT_FF_CHUNK, D_MODEL), lambda j, c, te, tv: (te[j], chunk(j, c, tv), 0))],
        out_specs=pl.BlockSpec((EXPERT_TILE, D_MODEL), lambda j, c, te, tv: (j, 0)),
        scratch_shapes=[pltpu.VMEM((EXPERT_TILE, D_MODEL), BF16), pltpu.VMEM((EXPERT_TILE, D_MODEL), F32)],
    )
    return pl.pallas_call(
        _expert_kernel,
        grid_spec=grid_spec,
        out_shape=jax.ShapeDtypeStruct((n_slots, D_MODEL), F32),
        compiler_params=_params(("arbitrary", "arbitrary")),
        name="moe_experts",
    )(tile_expert, tile_valid, xs, w1, w3, w2)


def _combine_kernel(dest_ref, x_ref, gate_ref, y_ref, o_ref, buf_ref, sem):
    tile = x_ref.shape[0]

    def copy(r, k):
        return pltpu.make_async_copy(y_ref.at[pl.ds(dest_ref[k, r], 1)], buf_ref.at[k, pl.ds(r, 1)], sem)

    def start(r, carry):
        for k in range(TOP_K):
            copy(r, k).start()
        return carry

    def wait(r, carry):
        for k in range(TOP_K):
            copy(r, k).wait()
        return carry

    lax.fori_loop(0, tile, start, 0)
    lax.fori_loop(0, tile, wait, 0)
    gates = gate_ref[...]
    o_ref[...] = x_ref[...] + gates[:, 0:1] * buf_ref[0] + gates[:, 1:2] * buf_ref[1]


def _combine(dest_tiles, x3, gates, y):
    n_tiles, _, tile = dest_tiles.shape
    n = x3.shape[0]
    return pl.pallas_call(
        _combine_kernel,
        grid=(n_tiles,),
        in_specs=[pl.BlockSpec((None, TOP_K, tile), lambda i: (i, 0, 0), memory_space=pltpu.SMEM),
                  _row_spec(tile, D_MODEL), _row_spec(tile, 128), pl.BlockSpec(memory_space=pl.ANY)],
        out_specs=_row_spec(tile, D_MODEL),
        out_shape=jax.ShapeDtypeStruct((n, D_MODEL), F32),
        scratch_shapes=[pltpu.VMEM((TOP_K, tile, D_MODEL), F32), pltpu.SemaphoreType.DMA],
        compiler_params=_params(("arbitrary",)),
        name="moe_combine",
    )(dest_tiles, x3, gates, y)


def _route(idx_all):
    n = idx_all.shape[0]
    e_flat = idx_all.reshape(-1)
    onehot = (e_flat[:, None] == jnp.arange(N_EXPERTS, dtype=jnp.int32)[None, :]).astype(jnp.int32)
    csum = jnp.cumsum(onehot, axis=0)
    counts = csum[-1]
    rank = jnp.sum(csum * onehot, axis=1) - 1
    padded = (counts + EXPERT_TILE - 1) // EXPERT_TILE * EXPERT_TILE
    pad_end = jnp.cumsum(padded)
    pad_start = pad_end - padded
    dest = (pad_start[e_flat] + rank).reshape(n, TOP_K).astype(jnp.int32)
    n_tiles = n * TOP_K // EXPERT_TILE + N_EXPERTS
    tile_start = jnp.arange(n_tiles, dtype=jnp.int32) * EXPERT_TILE
    tile_expert = jnp.clip(jnp.searchsorted(pad_end, tile_start, side='right'), 0, N_EXPERTS - 1)
    tile_valid = (tile_start < pad_end[-1]).astype(jnp.int32)
    return dest, tile_expert.astype(jnp.int32), tile_valid, n_tiles


def _trunk(groups, meta_tokens, norm_mix_e, w_in, q_gain, k_gain, rel_bias, conv_w, conv_b,
           conv_ln_g, conv_ln_b, w_out, norm_ffn_e, ffn_w1, ffn_w3, ffn_w2,
           norm_mix_o, pool_w, pool_scale, norm_ffn_o, router_w, moe_w1, moe_w3, moe_w2):
    row = lambda a: a.reshape(1, -1).astype(F32)
    shapes = [(g.shape[0], g.shape[1]) for g in groups]
    xs = [g.reshape(-1, D_MODEL) for g in groups]

    hsum = jnp.asarray(np.kron(np.eye(N_HEADS), np.ones((HEAD_DIM, HEAD_DIM))), BF16)
    qg = row(jnp.tile(q_gain[0], N_HEADS))
    kg = row(jnp.tile(k_gain[0], N_HEADS))
    w_in_b = w_in[0].astype(BF16)
    g_mix = row(norm_mix_e[0])
    bias_pat = _attn_bias_patterns(rel_bias[0])
    conv_w_p = jnp.concatenate([conv_w[0].astype(F32), jnp.zeros((1, CONV_DIM), F32)], axis=0)
    w_out_a = w_out[0, :ATTN_DIM].astype(BF16)
    w_out_c = w_out[0, ATTN_DIM:].astype(BF16)
    g_ffn = row(norm_ffn_e[0])
    w1b, w3b, w2b = ffn_w1[0].astype(BF16), ffn_w3[0].astype(BF16), ffn_w2[0].astype(BF16)

    meta = meta_tokens.astype(F32)
    qm, km, vm, cm = _proj(meta, g_mix, w_in_b, qg, kg, hsum, N_META)
    attn_m = _meta_attention(qm, km, vm)

    x2s, x2ms = [], []
    for x, (bsz, seq) in zip(xs, shapes):
        q, k, v, c = _proj(x, g_mix, w_in_b, qg, kg, hsum, TOKEN_TILE)
        attn = _attention(q, k, v, km, vm, bias_pat, bsz, seq)
        cact, cact_m = _conv(c, cm, conv_w_p, row(conv_b[0]), row(conv_ln_g[0]), row(conv_ln_b[0]),
                             bsz, seq, TOKEN_TILE)
        x1 = _out_proj(x, attn, cact, w_out_a, w_out_c, TOKEN_TILE)
        x1m = _out_proj(jnp.tile(meta, (bsz, 1)), jnp.tile(attn_m, (bsz, 1)), cact_m, w_out_a, w_out_c,
                        bsz * N_META)
        x2s.append(_ffn(x1, g_ffn, w1b, w3b, w2b, TOKEN_TILE))
        x2ms.append(_ffn(x1m, g_ffn, w1b, w3b, w2b, bsz * N_META))

    g_pool = row(norm_mix_o[0])
    w_pool_b = pool_w[0].astype(BF16)
    g_moe = row(norm_ffn_o[0])
    rw = jnp.zeros((D_MODEL, 128), F32).at[:, :N_EXPERTS].set(router_w[0].astype(F32))
    r_hi = rw.astype(BF16)
    r_lo = (rw - r_hi.astype(F32)).astype(BF16)
    mw1, mw3, mw2 = moe_w1[0].astype(BF16), moe_w3[0].astype(BF16), moe_w2[0].astype(BF16)

    x3s, hns, idxs, gates = [], [], [], []
    for x2, x2m, (bsz, seq) in zip(x2s, x2ms, shapes):
        x3, hn, idx, gate = _pool_router(x2, x2m, g_pool, w_pool_b, row(pool_scale[0]), g_moe, r_hi, r_lo,
                                         bsz, seq, TOKEN_TILE)
        x3s.append(x3); hns.append(hn); idxs.append(idx[:, :TOP_K]); gates.append(gate)

    dest, tile_expert, tile_valid, n_tiles = _route(jnp.concatenate(idxs, axis=0))
    slots = jnp.zeros((n_tiles * EXPERT_TILE, D_MODEL), F32)
    dests, off = [], 0
    for hn in hns:
        n = hn.shape[0]
        d = dest[off:off + n].reshape(n // DMA_TILE, DMA_TILE, TOP_K).transpose(0, 2, 1)
        dests.append(d)
        slots = _dispatch(d, hn, slots)
        off += n
    y = _experts(tile_expert, tile_valid, slots, mw1, mw3, mw2)
    outs = []
    for x3, gate, d, (bsz, seq) in zip(x3s, gates, dests, shapes):
        outs.append(_combine(d, x3, gate, y).reshape(bsz, seq, D_MODEL))
    return tuple(outs)


def kernel(x_prompt, x_sample, meta_tokens, norm_mix_e, w_in, q_gain, k_gain, rel_bias, conv_w, conv_b, conv_ln_g, conv_ln_b, w_out, norm_ffn_e, ffn_w1, ffn_w3, ffn_w2, norm_mix_o, pool_w, pool_scale, norm_ffn_o, router_w, moe_w1, moe_w3, moe_w2):
    return _trunk([x_prompt, x_sample], meta_tokens, norm_mix_e, w_in, q_gain, k_gain, rel_bias, conv_w,
                  conv_b, conv_ln_g, conv_ln_b, w_out, norm_ffn_e, ffn_w1, ffn_w3, ffn_w2,
                  norm_mix_o, pool_w, pool_scale, norm_ffn_o, router_w, moe_w1, moe_w3, moe_w2)
```

```python
import functools

import numpy as np
import jax
import jax.numpy as jnp
from jax import lax
from jax.experimental import pallas as pl
from jax.experimental.pallas import tpu as pltpu

D_MODEL = 1024
N_META = 16
GRID_W = 64
N_HEADS = 8
HEAD_DIM = 64
ATTN_DIM = N_HEADS * HEAD_DIM
CONV_DIM = D_MODEL - ATTN_DIM
IN_DIM = 3 * ATTN_DIM + 2 * CONV_DIM
WIN_ROWS = 8
WIN_COLS = 16
CONV_WIDTH = 31
CONV_HALF = CONV_WIDTH // 2
POOL_WINDOWS = (2, 4, 8, 16)
POOL_GROUP_DIM = D_MODEL // len(POOL_WINDOWS)
D_FF = 2816
N_EXPERTS = 8
TOP_K = 2
D_FF_EXPERT = 3584
EPS = 1e-6

F32 = jnp.float32
BF16 = jnp.bfloat16
SUBLANES = 8

TOKEN_TILE = 512
ATTN_ROWS = 4
ATTN_Q = ATTN_ROWS * GRID_W
ATTN_KROWS = ATTN_ROWS + WIN_ROWS
ATTN_KBLK = ATTN_KROWS * GRID_W // ATTN_Q
CONV_HALO = 16
CONV_CHUNK = 64
POOL_HALO = 8
EXPERT_TILE = 1024
EXPERT_FF_CHUNK = 512
MOE_TILE = 512
MOE_CHUNK = 8
MOE_ROWS = 1152
NEG_INF = -1e30
VMEM_LIMIT = 56 * 1024 * 1024


def _params(sem):
    return pltpu.CompilerParams(dimension_semantics=sem, vmem_limit_bytes=VMEM_LIMIT)


def _const_spec(shape):
    zeros = (0,) * len(shape)
    return pl.BlockSpec(shape, lambda *_: zeros)


def _row_spec(tm, cols):
    return pl.BlockSpec((tm, cols), lambda i: (i, 0))


def _rms(x, gain):
    ms = jnp.mean(x * x, axis=-1, keepdims=True)
    return x * lax.rsqrt(ms + EPS) * gain


def _proj_kernel(x_ref, g_ref, w_ref, qg_ref, kg_ref, hsum_ref, q_ref, k_ref, v_ref, c_ref):
    n = _rms(x_ref[...], g_ref[...]).astype(BF16)
    proj = jnp.dot(n, w_ref[...], preferred_element_type=F32)
    q = proj[:, :ATTN_DIM]
    k = proj[:, ATTN_DIM:2 * ATTN_DIM]
    v = proj[:, 2 * ATTN_DIM:3 * ATTN_DIM]
    u = proj[:, 3 * ATTN_DIM:3 * ATTN_DIM + CONV_DIM]
    g = proj[:, 3 * ATTN_DIM + CONV_DIM:]

    def head_norm(a, gain):
        ss = jnp.dot((a * a).astype(BF16), hsum_ref[...], preferred_element_type=F32)
        return a * lax.rsqrt(ss * (1.0 / HEAD_DIM) + EPS) * gain

    q_ref[...] = (head_norm(q, qg_ref[...]) * (HEAD_DIM ** -0.5)).astype(BF16)
    k_ref[...] = head_norm(k, kg_ref[...]).astype(BF16)
    v_ref[...] = v.astype(BF16)
    c_ref[...] = (u * jax.nn.sigmoid(g)).astype(BF16)


def _proj(x, norm_g, w_in, qg, kg, hsum, tm):
    n = x.shape[0]
    out = jax.ShapeDtypeStruct((n, ATTN_DIM), BF16)
    return pl.pallas_call(
        _proj_kernel,
        grid=(n // tm,),
        in_specs=[_row_spec(tm, D_MODEL), _const_spec((1, D_MODEL)), _const_spec((D_MODEL, IN_DIM)),
                  _const_spec((1, ATTN_DIM)), _const_spec((1, ATTN_DIM)), _const_spec((ATTN_DIM, ATTN_DIM))],
        out_specs=[_row_spec(tm, ATTN_DIM)] * 4,
        out_shape=[out] * 4,
        compiler_params=_params(("arbitrary",)),
        name="in_proj",
    )(x, norm_g, w_in, qg, kg, hsum)


def _attn_bias_patterns(rel_bias):
    n_dr, n_dc = 2 * WIN_ROWS - 1, 2 * WIN_COLS - 1
    qr = np.arange(ATTN_ROWS)[:, None]
    kr = np.arange(ATTN_KROWS)[None, :]
    qc = np.arange(GRID_W)[:, None]
    kc = np.arange(GRID_W)[None, :]
    cs = np.clip(qc - WIN_COLS // 2, 0, GRID_W - WIN_COLS)
    col_ok = (kc >= cs) & (kc < cs + WIN_COLS)
    dc = np.clip(kc - qc + (WIN_COLS - 1), 0, n_dc - 1)
    col_sel = (dc[None] == np.arange(n_dc)[:, None, None]).astype(np.float32)
    row_sel, ok = [], []
    for off, rs in ((0, 0 * qr), (ATTN_ROWS, qr), (2 * ATTN_ROWS, ATTN_ROWS + 0 * qr)):
        row_ok = (kr >= rs) & (kr < rs + WIN_ROWS)
        dr = np.clip(kr - (off + qr) + (WIN_ROWS - 1), 0, n_dr - 1)
        row_sel.append((dr[..., None] == np.arange(n_dr)).astype(np.float32))
        ok.append(row_ok[:, None, :, None] & col_ok[None, :, None, :])
    row_sel = np.stack(row_sel)
    ok = np.stack(ok).reshape(3, 1, ATTN_Q, ATTN_KROWS * GRID_W)
    vals = jnp.einsum('pqka,hab,bcd->phqckd', row_sel, rel_bias.astype(F32), col_sel,
                      precision=lax.Precision.HIGHEST)
    vals = vals.reshape(3, N_HEADS, ATTN_Q, ATTN_KROWS * GRID_W)
    return jnp.where(ok, vals, NEG_INF)


def _attn_kernel(q_ref, k0_ref, k1_ref, k2_ref, v0_ref, v1_ref, v2_ref, km_ref, vm_ref, bias_ref, o_ref):
    nt = (((1,), (1,)), ((), ()))
    k_refs = (k0_ref, k1_ref, k2_ref)
    v_refs = (v0_ref, v1_ref, v2_ref)
    outs = []
    for h in range(N_HEADS):
        sl = slice(h * HEAD_DIM, (h + 1) * HEAD_DIM)
        qh = q_ref[:, sl]
        s_meta = lax.dot_general(qh, km_ref[:, sl], nt, preferred_element_type=F32)
        s_loc = [lax.dot_general(qh, k_refs[t][:, sl], nt, preferred_element_type=F32)
                 + bias_ref[0, h, :, t * ATTN_Q:(t + 1) * ATTN_Q] for t in range(ATTN_KBLK)]
        m_loc = functools.reduce(jnp.maximum, s_loc)
        m = jnp.maximum(jnp.max(s_meta, axis=-1, keepdims=True), jnp.max(m_loc, axis=-1, keepdims=True))
        p_meta = jnp.exp(s_meta - m)
        acc = jnp.dot(p_meta.astype(BF16), vm_ref[:, sl], preferred_element_type=F32)
        p_sum = None
        for t in range(ATTN_KBLK):
            p = jnp.exp(s_loc[t] - m)
            p_sum = p if p_sum is None else p_sum + p
            acc = acc + jnp.dot(p.astype(BF16), v_refs[t][:, sl], preferred_element_type=F32)
        denom = jnp.sum(p_meta, axis=-1, keepdims=True) + jnp.sum(p_sum, axis=-1, keepdims=True)
        outs.append(acc / denom)
    o_ref[...] = jnp.concatenate(outs, axis=-1).astype(BF16)


def _attention(q, k, v, km, vm, bias_pat, bsz, seq):
    nb = seq // ATTN_Q
    assert seq % ATTN_Q == 0 and nb >= ATTN_KBLK

    def q_map(b, j):
        return (b * nb + j, 0)

    def kv_map(t):
        return lambda b, j: (b * nb + jnp.clip(j - 1, 0, nb - ATTN_KBLK) + t, 0)

    def bias_map(b, j):
        return (jnp.where(j == 0, 0, jnp.where(j == nb - 1, 2, 1)), 0, 0, 0)

    blk = (ATTN_Q, ATTN_DIM)
    kv_specs = [pl.BlockSpec(blk, kv_map(t)) for t in range(ATTN_KBLK)]
    return pl.pallas_call(
        _attn_kernel,
        grid=(bsz, nb),
        in_specs=[pl.BlockSpec(blk, q_map)] + kv_specs + kv_specs
                 + [_const_spec((N_META, ATTN_DIM)), _const_spec((N_META, ATTN_DIM)),
                    pl.BlockSpec((1, N_HEADS, ATTN_Q, ATTN_KBLK * ATTN_Q), bias_map)],
        out_specs=pl.BlockSpec(blk, q_map),
        out_shape=jax.ShapeDtypeStruct((bsz * seq, ATTN_DIM), BF16),
        compiler_params=_params(("arbitrary", "arbitrary")),
        name="nbr_attn",
    )(q, k, k, k, v, v, v, km, vm, bias_pat)


def _meta_attn_kernel(q_ref, k_ref, v_ref, o_ref):
    nt = (((1,), (1,)), ((), ()))
    outs = []
    for h in range(N_HEADS):
        sl = slice(h * HEAD_DIM, (h + 1) * HEAD_DIM)
        s = lax.dot_general(q_ref[:, sl], k_ref[:, sl], nt, preferred_element_type=F32)
        p = jnp.exp(s - jnp.max(s, axis=-1, keepdims=True))
        o = jnp.dot(p.astype(BF16), v_ref[:, sl], preferred_element_type=F32)
        outs.append(o / jnp.sum(p, axis=-1, keepdims=True))
    o_ref[...] = jnp.concatenate(outs, axis=-1).astype(BF16)


def _meta_attention(qm, km, vm):
    spec = _const_spec((N_META, ATTN_DIM))
    return pl.pallas_call(
        _meta_attn_kernel,
        in_specs=[spec] * 3, out_specs=spec, grid=(1,),
        out_shape=jax.ShapeDtypeStruct((N_META, ATTN_DIM), BF16),
        compiler_params=_params(("arbitrary",)),
        name="meta_attn",
    )(qm, km, vm)


def _conv_kernel(c_ref, prev_ref, next_ref, cm_ref, w_ref, b_ref, lg_ref, lb_ref,
                 out_ref, outm_ref, xs_ref, mpad_ref):
    tile = c_ref.shape[0]
    n_pad = tile + 2 * CONV_HALO
    i = pl.program_id(1)
    first = i == 0
    last = i == pl.num_programs(1) - 1
    cm = cm_ref[...].astype(F32)
    xs_ref[0, 0:CONV_HALO, :] = jnp.where(first, cm, prev_ref[...].astype(F32))
    xs_ref[0, CONV_HALO:CONV_HALO + tile, :] = c_ref[...].astype(F32)
    xs_ref[0, CONV_HALO + tile:, :] = jnp.where(last, 0.0, next_ref[...].astype(F32))
    for s in range(1, SUBLANES):
        xs_ref[s, 0:n_pad - SUBLANES, :] = xs_ref[0, s:s + n_pad - SUBLANES, :]

    def conv_rows(read, rows):
        acc = jnp.zeros((rows, CONV_DIM), F32)
        for j in range(CONV_WIDTH):
            acc = acc + w_ref[j:j + 1, :] * read(j - CONV_HALF)
        y = acc + b_ref[...]
        mu = jnp.mean(y, axis=-1, keepdims=True)
        yc = y - mu
        var = jnp.mean(yc * yc, axis=-1, keepdims=True)
        z = yc * lax.rsqrt(var + EPS) * lg_ref[...] + lb_ref[...]
        return (z * jax.nn.sigmoid(z)).astype(BF16)

    for rc in range(tile // CONV_CHUNK):
        start = CONV_HALO + rc * CONV_CHUNK

        def read(d, start=start):
            lo = start + d
            return xs_ref[lo % SUBLANES, lo - lo % SUBLANES:lo - lo % SUBLANES + CONV_CHUNK, :]

        out_ref[rc * CONV_CHUNK:(rc + 1) * CONV_CHUNK, :] = conv_rows(read, CONV_CHUNK)

    @pl.when(first)
    def _():
        mpad_ref[0:N_META, :] = jnp.zeros((N_META, CONV_DIM), F32)
        mpad_ref[N_META:2 * N_META, :] = cm
        mpad_ref[2 * N_META:, :] = c_ref[0:N_META, :].astype(F32)
        outm_ref[...] = conv_rows(lambda d: mpad_ref[N_META + d:2 * N_META + d, :], N_META)


def _conv(c, cm, conv_w, conv_b, ln_g, ln_b, bsz, seq, tile):
    tps = seq // tile
    hpt = tile // CONV_HALO
    n_halo = bsz * seq // CONV_HALO

    def main_map(b, i):
        return (b * tps + i, 0)

    def prev_map(b, i):
        return (jnp.maximum((b * tps + i) * hpt - 1, 0), 0)

    def next_map(b, i):
        return (jnp.minimum((b * tps + i + 1) * hpt, n_halo - 1), 0)

    halo = (CONV_HALO, CONV_DIM)
    vec = _const_spec((1, CONV_DIM))
    return pl.pallas_call(
        _conv_kernel,
        grid=(bsz, tps),
        in_specs=[pl.BlockSpec((tile, CONV_DIM), main_map), pl.BlockSpec(halo, prev_map),
                  pl.BlockSpec(halo, next_map), _const_spec((N_META, CONV_DIM)),
                  _const_spec((CONV_WIDTH + 1, CONV_DIM)), vec, vec, vec],
        out_specs=[pl.BlockSpec((tile, CONV_DIM), main_map),
                   pl.BlockSpec((N_META, CONV_DIM), lambda b, i: (b, 0))],
        out_shape=[jax.ShapeDtypeStruct((bsz * seq, CONV_DIM), BF16),
                   jax.ShapeDtypeStruct((bsz * N_META, CONV_DIM), BF16)],
        scratch_shapes=[pltpu.VMEM((SUBLANES, tile + 2 * CONV_HALO, CONV_DIM), F32),
                        pltpu.VMEM((3 * N_META, CONV_DIM), F32)],
        compiler_params=_params(("arbitrary", "arbitrary")),
        name="conv_ln_silu",
    )(c, c, c, cm, conv_w, conv_b, ln_g, ln_b)


def _out_proj_kernel(x_ref, a_ref, c_ref, wa_ref, wc_ref, o_ref):
    o_ref[...] = (x_ref[...]
                  + jnp.dot(a_ref[...], wa_ref[...], preferred_element_type=F32)
                  + jnp.dot(c_ref[...], wc_ref[...], preferred_element_type=F32))


def _out_proj(x, attn, cact, w_attn, w_conv, tm):
    n = x.shape[0]
    wspec = _const_spec((ATTN_DIM, D_MODEL))
    return pl.pallas_call(
        _out_proj_kernel,
        grid=(n // tm,),
        in_specs=[_row_spec(tm, D_MODEL), _row_spec(tm, ATTN_DIM), _row_spec(tm, CONV_DIM), wspec, wspec],
        out_specs=_row_spec(tm, D_MODEL),
        out_shape=jax.ShapeDtypeStruct((n, D_MODEL), F32),
        compiler_params=_params(("arbitrary",)),
        name="out_proj",
    )(x, attn, cact, w_attn, w_conv)


def _ffn_kernel(x_ref, g_ref, w1_ref, w3_ref, w2_ref, o_ref):
    x = x_ref[...]
    n = _rms(x, g_ref[...]).astype(BF16)
    h1 = jnp.dot(n, w1_ref[...], preferred_element_type=F32)
    h3 = jnp.dot(n, w3_ref[...], preferred_element_type=F32)
    act = (h1 * jax.nn.sigmoid(h1) * h3).astype(BF16)
    o_ref[...] = x + jnp.dot(act, w2_ref[...], preferred_element_type=F32)


def _ffn(x, g, w1, w3, w2, tm):
    n = x.shape[0]
    return pl.pallas_call(
        _ffn_kernel,
        grid=(n // tm,),
        in_specs=[_row_spec(tm, D_MODEL), _const_spec((1, D_MODEL)), _const_spec((D_MODEL, D_FF)),
                  _const_spec((D_MODEL, D_FF)), _const_spec((D_FF, D_MODEL))],
        out_specs=_row_spec(tm, D_MODEL),
        out_shape=jax.ShapeDtypeStruct((n, D_MODEL), F32),
        compiler_params=_params(("arbitrary",)),
        name="dense_swiglu",
    )(x, g, w1, w3, w2)


def _pool_router_kernel(x_ref, prev_ref, next_ref, xm_ref, g_ref, wp_ref, ps_ref, g2_ref, rhi_ref, rlo_ref,
                        x3_ref, hn_ref, idx_ref, gate_ref, npad_ref, *, seq):
    tile = x_ref.shape[0]
    i = pl.program_id(1)
    first = i == 0
    last = i == pl.num_programs(1) - 1
    gain = g_ref[...]
    x = x_ref[...]
    n_main = _rms(x, gain)
    npad_ref[0:POOL_HALO, :] = _rms(jnp.where(first, xm_ref[...], prev_ref[...]), gain)
    npad_ref[POOL_HALO:POOL_HALO + tile, :] = n_main
    npad_ref[POOL_HALO + tile:, :] = jnp.where(last, 0.0, _rms(next_ref[...], gain))

    tok = i * tile + lax.broadcasted_iota(jnp.int32, (tile, 1), 0)
    mixed = []
    for gi, w in enumerate(POOL_WINDOWS):
        cols = slice(gi * POOL_GROUP_DIM, (gi + 1) * POOL_GROUP_DIM)
        half = w // 2
        acc = jnp.zeros((tile, POOL_GROUP_DIM), F32)
        for d in range(-half, half):
            acc = acc + npad_ref[POOL_HALO + d:POOL_HALO + d + tile, cols]
        count = (w - jnp.maximum(tok + half - seq, 0)).astype(F32)
        diff = (acc / count - n_main[:, cols]).astype(BF16)
        mixed.append(jnp.dot(diff, wp_ref[gi], preferred_element_type=F32))
    x3 = x + jnp.concatenate(mixed, axis=-1) * ps_ref[...]
    x3_ref[...] = x3

    hn = _rms(x3, g2_ref[...])
    hn_ref[...] = hn
    h_hi = hn.astype(BF16)
    h_lo = (hn - h_hi.astype(F32)).astype(BF16)
    logits = (jnp.dot(h_hi, rhi_ref[...], preferred_element_type=F32)
              + jnp.dot(h_lo, rhi_ref[...], preferred_element_type=F32)
              + jnp.dot(h_hi, rlo_ref[...], preferred_element_type=F32))
    lane = lax.broadcasted_iota(jnp.int32, logits.shape, 1)
    lane_f = lane.astype(F32)
    logits = jnp.where(lane < N_EXPERTS, logits, NEG_INF)
    m1 = jnp.max(logits, axis=-1, keepdims=True)
    i1 = jnp.min(jnp.where(logits == m1, lane_f, 256.0), axis=-1, keepdims=True)
    rest = jnp.where(lane_f == i1, NEG_INF, logits)
    m2 = jnp.max(rest, axis=-1, keepdims=True)
    i2 = jnp.min(jnp.where(rest == m2, lane_f, 256.0), axis=-1, keepdims=True)
    e2 = jnp.exp(m2 - m1)
    g1 = 1.0 / (1.0 + e2)
    idx_ref[...] = jnp.where(lane == 0, i1, jnp.where(lane == 1, i2, 0.0)).astype(jnp.int32)
    gate_ref[...] = jnp.where(lane == 0, g1, jnp.where(lane == 1, e2 * g1, 0.0))


def _pool_router(x2, x2m, g, w_pool, pool_scale, g2, r_hi, r_lo, bsz, seq, tile):
    tps = seq // tile
    hpt = tile // POOL_HALO
    n_halo = bsz * seq // POOL_HALO

    def main_map(b, i):
        return (b * tps + i, 0)

    def prev_map(b, i):
        return (jnp.maximum((b * tps + i) * hpt - 1, 0), 0)

    def next_map(b, i):
        return (jnp.minimum((b * tps + i + 1) * hpt, n_halo - 1), 0)

    halo = (POOL_HALO, D_MODEL)
    vec = _const_spec((1, D_MODEL))
    main = pl.BlockSpec((tile, D_MODEL), main_map)
    small = pl.BlockSpec((tile, 128), main_map)
    n = bsz * seq
    return pl.pallas_call(
        functools.partial(_pool_router_kernel, seq=seq),
        grid=(bsz, tps),
        in_specs=[main, pl.BlockSpec(halo, prev_map), pl.BlockSpec(halo, next_map),
                  pl.BlockSpec(halo, lambda b, i: (2 * b + 1, 0)), vec,
                  _const_spec((len(POOL_WINDOWS), POOL_GROUP_DIM, POOL_GROUP_DIM)), vec, vec,
                  _const_spec((D_MODEL, 128)), _const_spec((D_MODEL, 128))],
        out_specs=[main, main, small, small],
        out_shape=[jax.ShapeDtypeStruct((n, D_MODEL), F32), jax.ShapeDtypeStruct((n, D_MODEL), F32),
                   jax.ShapeDtypeStruct((n, 128), jnp.int32), jax.ShapeDtypeStruct((n, 128), F32)],
        scratch_shapes=[pltpu.VMEM((tile + 2 * POOL_HALO, D_MODEL), F32)],
        compiler_params=_params(("arbitrary", "arbitrary")),
        name="pool_router",
    )(x2, x2, x2, x2m, g, w_pool, pool_scale, g2, r_hi, r_lo)


def _segment_copies(copy, off_ref, slot_ref, nch_ref):
    base = pl.program_id(0) * N_EXPERTS
    for e in range(N_EXPERTS):
        off = off_ref[base + e]
        slot = slot_ref[base + e]

        def start(c, carry):
            copy(pl.multiple_of(off + c * MOE_CHUNK, MOE_CHUNK),
                 pl.multiple_of(slot + c * MOE_CHUNK, MOE_CHUNK)).start()
            return carry

        lax.fori_loop(0, nch_ref[base + e], start, 0)
    for e in range(N_EXPERTS):
        def wait(c, carry):
            copy(0, 0).wait()
            return carry

        lax.fori_loop(0, nch_ref[base + e], wait, 0)


def _dispatch_kernel(off_ref, slot_ref, nch_ref, pos_ref, hn_ref, xs_in_ref, xs_ref, buf_ref, sem):
    del xs_in_ref
    rows = lax.broadcasted_iota(jnp.int32, (MOE_ROWS, MOE_TILE), 0)
    perm = jnp.where(rows == pos_ref[0:1, :], 1.0, jnp.where(rows == pos_ref[1:2, :], 1.0, 0.0))
    buf_ref[...] = jnp.dot(perm.astype(BF16), hn_ref[...].astype(BF16), preferred_element_type=F32)

    def copy(buf_row, slot_row):
        return pltpu.make_async_copy(buf_ref.at[pl.ds(buf_row, MOE_CHUNK)],
                                     xs_ref.at[pl.ds(slot_row, MOE_CHUNK)], sem)

    _segment_copies(copy, off_ref, slot_ref, nch_ref)


def _dispatch(seg_off, seg_slot, seg_nch, pos_t, hn, xs):
    n = hn.shape[0]
    grid_spec = pltpu.PrefetchScalarGridSpec(
        num_scalar_prefetch=3,
        grid=(n // MOE_TILE,),
        in_specs=[pl.BlockSpec((TOP_K, MOE_TILE), lambda i, *_: (0, i)),
                  pl.BlockSpec((MOE_TILE, D_MODEL), lambda i, *_: (i, 0)),
                  pl.BlockSpec(memory_space=pl.ANY)],
        out_specs=pl.BlockSpec(memory_space=pl.ANY),
        scratch_shapes=[pltpu.VMEM((MOE_ROWS, D_MODEL), F32), pltpu.SemaphoreType.DMA],
    )
    return pl.pallas_call(
        _dispatch_kernel,
        grid_spec=grid_spec,
        out_shape=jax.ShapeDtypeStruct(xs.shape, xs.dtype),
        input_output_aliases={5: 0},
        compiler_params=_params(("arbitrary",)),
        name="moe_dispatch",
    )(seg_off, seg_slot, seg_nch, pos_t, hn, xs)


def _expert_kernel(te_ref, tv_ref, x_ref, w1_ref, w3_ref, w2_ref, y_ref, xb_ref, act_ref):
    del te_ref
    j = pl.program_id(0)
    c = pl.program_id(1)
    n_chunks = act_ref.shape[0]
    valid = tv_ref[j] == 1
    last = c == n_chunks - 1

    @pl.when(valid & (c == 0))
    def _():
        xb_ref[...] = x_ref[...].astype(BF16)

    @pl.when(valid)
    def _():
        xb = xb_ref[...]
        h1 = jnp.dot(xb, w1_ref[...], preferred_element_type=F32)
        h3 = jnp.dot(xb, w3_ref[...], preferred_element_type=F32)
        act_ref[c] = (h1 * jax.nn.sigmoid(h1) * h3).astype(BF16)

    @pl.when(valid & last)
    def _():
        acc = jnp.dot(act_ref[0], w2_ref[0:EXPERT_FF_CHUNK, :], preferred_element_type=F32)
        for cc in range(1, n_chunks):
            acc = acc + jnp.dot(act_ref[cc], w2_ref[cc * EXPERT_FF_CHUNK:(cc + 1) * EXPERT_FF_CHUNK, :],
                                preferred_element_type=F32)
        y_ref[...] = acc

    @pl.when(jnp.logical_not(valid) & last)
    def _():
        y_ref[...] = jnp.zeros(y_ref.shape, F32)


def _experts(tile_expert, tile_valid, xs, w1, w3, w2):
    n_slots = xs.shape[0]
    n_tiles = n_slots // EXPERT_TILE
    n_chunks = D_FF_EXPERT // EXPERT_FF_CHUNK

    def chunk(j, c, tv):
        return jnp.where(tv[j] == 1, c, n_chunks - 1)

    grid_spec = pltpu.PrefetchScalarGridSpec(
        num_scalar_prefetch=2,
        grid=(n_tiles, n_chunks),
        in_specs=[pl.BlockSpec((EXPERT_TILE, D_MODEL), lambda j, c, te, tv: (j, 0)),
                  pl.BlockSpec((None, D_MODEL, EXPERT_FF_CHUNK), lambda j, c, te, tv: (te[j], 0, chunk(j, c, tv))),
                  pl.BlockSpec((None, D_MODEL, EXPERT_FF_CHUNK), lambda j, c, te, tv: (te[j], 0, chunk(j, c, tv))),
                  pl.BlockSpec((None, D_FF_EXPERT, D_MODEL), lambda j, c, te, tv: (te[j], 0, 0),
                               pipeline_mode=pl.Buffered(1))],
        out_specs=pl.BlockSpec((EXPERT_TILE, D_MODEL), lambda j, c, te, tv: (j, 0)),
        scratch_shapes=[pltpu.VMEM((EXPERT_TILE, D_MODEL), BF16),
                        pltpu.VMEM((n_chunks, EXPERT_TILE, EXPERT_FF_CHUNK), BF16)],
    )
    return pl.pallas_call(
        _expert_kernel,
        grid_spec=grid_spec,
        out_shape=jax.ShapeDtypeStruct((n_slots, D_MODEL), F32),
        compiler_params=_params(("arbitrary", "arbitrary")),
        name="moe_experts",
    )(tile_expert, tile_valid, xs, w1, w3, w2)


def _combine_kernel(off_ref, slot_ref, nch_ref, pos_ref, x_ref, gate_ref, y_ref, o_ref, buf_ref, sem):
    @pl.when(pl.program_id(0) == 0)
    def _():
        buf_ref[...] = jnp.zeros(buf_ref.shape, F32)

    def copy(buf_row, slot_row):
        return pltpu.make_async_copy(y_ref.at[pl.ds(slot_row, MOE_CHUNK)],
                                     buf_ref.at[pl.ds(buf_row, MOE_CHUNK)], sem)

    _segment_copies(copy, off_ref, slot_ref, nch_ref)
    yb = buf_ref[...].astype(BF16)
    cols = lax.broadcasted_iota(jnp.int32, (MOE_TILE, MOE_ROWS), 1)
    pos = pos_ref[...]
    gates = gate_ref[...]
    out = x_ref[...]
    for k in range(TOP_K):
        pick = jnp.where(cols == pos[:, k:k + 1], 1.0, 0.0).astype(BF16)
        out = out + gates[:, k:k + 1] * jnp.dot(pick, yb, preferred_element_type=F32)
    o_ref[...] = out


def _combine(seg_off, seg_slot, seg_nch, pos, x3, gates, y):
    n = x3.shape[0]
    grid_spec = pltpu.PrefetchScalarGridSpec(
        num_scalar_prefetch=3,
        grid=(n // MOE_TILE,),
        in_specs=[pl.BlockSpec((MOE_TILE, TOP_K), lambda i, *_: (i, 0)),
                  pl.BlockSpec((MOE_TILE, D_MODEL), lambda i, *_: (i, 0)),
                  pl.BlockSpec((MOE_TILE, 128), lambda i, *_: (i, 0)),
                  pl.BlockSpec(memory_space=pl.ANY)],
        out_specs=pl.BlockSpec((MOE_TILE, D_MODEL), lambda i, *_: (i, 0)),
        scratch_shapes=[pltpu.VMEM((MOE_ROWS, D_MODEL), F32), pltpu.SemaphoreType.DMA],
    )
    return pl.pallas_call(
        _combine_kernel,
        grid_spec=grid_spec,
        out_shape=jax.ShapeDtypeStruct((n, D_MODEL), F32),
        compiler_params=_params(("arbitrary",)),
        name="moe_combine",
    )(seg_off, seg_slot, seg_nch, pos, x3, gates, y)


def _route(idx_all):
    n = idx_all.shape[0]
    per_tile = TOP_K * MOE_TILE
    e_flat = idx_all.reshape(-1)
    onehot = (e_flat[:, None] == jnp.arange(N_EXPERTS, dtype=jnp.int32)[None, :]).astype(jnp.int32)
    csum = jnp.cumsum(onehot, axis=0)
    rank = jnp.sum(csum * onehot, axis=1) - 1
    tile_end = csum[per_tile - 1::per_tile]
    tile_base = jnp.concatenate([jnp.zeros((1, N_EXPERTS), jnp.int32), tile_end[:-1]], axis=0)
    seg = (tile_end - tile_base + MOE_CHUNK - 1) // MOE_CHUNK * MOE_CHUNK
    seg_off = jnp.cumsum(seg, axis=1) - seg
    seg_nch = seg // MOE_CHUNK
    padded = (jnp.sum(seg, axis=0) + EXPERT_TILE - 1) // EXPERT_TILE * EXPERT_TILE
    pad_end = jnp.cumsum(padded)
    seg_slot = (pad_end - padded)[None, :] + jnp.cumsum(seg, axis=0) - seg
    rel = jnp.repeat(seg_off - tile_base, per_tile, axis=0)
    pos = (jnp.sum(rel * onehot, axis=1) + rank).reshape(n, TOP_K)
    worst_rows = n * TOP_K + N_EXPERTS * (MOE_CHUNK - 1) * (n // MOE_TILE) + N_EXPERTS * (EXPERT_TILE - 1)
    n_tiles = (worst_rows + EXPERT_TILE - 1) // EXPERT_TILE
    tile_start = jnp.arange(n_tiles, dtype=jnp.int32) * EXPERT_TILE
    tile_expert = jnp.sum((tile_start[:, None] >= pad_end[None, :]).astype(jnp.int32), axis=1)
    tile_expert = jnp.minimum(tile_expert, N_EXPERTS - 1)
    tile_valid = (tile_start < pad_end[-1]).astype(jnp.int32)
    return seg_off, seg_slot, seg_nch, pos, tile_expert, tile_valid, n_tiles


def _trunk(groups, meta_tokens, norm_mix_e, w_in, q_gain, k_gain, rel_bias, conv_w, conv_b,
           conv_ln_g, conv_ln_b, w_out, norm_ffn_e, ffn_w1, ffn_w3, ffn_w2,
           norm_mix_o, pool_w, pool_scale, norm_ffn_o, router_w, moe_w1, moe_w3, moe_w2):
    row = lambda a: a.reshape(1, -1).astype(F32)
    shapes = [(g.shape[0], g.shape[1]) for g in groups]
    xs = [g.reshape(-1, D_MODEL) for g in groups]

    hsum = jnp.asarray(np.kron(np.eye(N_HEADS), np.ones((HEAD_DIM, HEAD_DIM))), BF16)
    qg = row(jnp.tile(q_gain[0], N_HEADS))
    kg = row(jnp.tile(k_gain[0], N_HEADS))
    w_in_b = w_in[0].astype(BF16)
    g_mix = row(norm_mix_e[0])
    bias_pat = _attn_bias_patterns(rel_bias[0])
    conv_w_p = jnp.concatenate([conv_w[0].astype(F32), jnp.zeros((1, CONV_DIM), F32)], axis=0)
    w_out_a = w_out[0, :ATTN_DIM].astype(BF16)
    w_out_c = w_out[0, ATTN_DIM:].astype(BF16)
    g_ffn = row(norm_ffn_e[0])
    w1b, w3b, w2b = ffn_w1[0].astype(BF16), ffn_w3[0].astype(BF16), ffn_w2[0].astype(BF16)

    meta = meta_tokens.astype(F32)
    qm, km, vm, cm = _proj(meta, g_mix, w_in_b, qg, kg, hsum, N_META)
    attn_m = _meta_attention(qm, km, vm)

    x2s, x2ms = [], []
    for x, (bsz, seq) in zip(xs, shapes):
        q, k, v, c = _proj(x, g_mix, w_in_b, qg, kg, hsum, TOKEN_TILE)
        attn = _attention(q, k, v, km, vm, bias_pat, bsz, seq)
        cact, cact_m = _conv(c, cm, conv_w_p, row(conv_b[0]), row(conv_ln_g[0]), row(conv_ln_b[0]),
                             bsz, seq, TOKEN_TILE)
        x1 = _out_proj(x, attn, cact, w_out_a, w_out_c, TOKEN_TILE)
        x1m = _out_proj(jnp.tile(meta, (bsz, 1)), jnp.tile(attn_m, (bsz, 1)), cact_m, w_out_a, w_out_c,
                        bsz * N_META)
        x2s.append(_ffn(x1, g_ffn, w1b, w3b, w2b, TOKEN_TILE))
        x2ms.append(_ffn(x1m, g_ffn, w1b, w3b, w2b, bsz * N_META))

    g_pool = row(norm_mix_o[0])
    w_pool_b = pool_w[0].astype(BF16)
    g_moe = row(norm_ffn_o[0])
    rw = jnp.zeros((D_MODEL, 128), F32).at[:, :N_EXPERTS].set(router_w[0].astype(F32))
    r_hi = rw.astype(BF16)
    r_lo = (rw - r_hi.astype(F32)).astype(BF16)
    mw1, mw3, mw2 = moe_w1[0].astype(BF16), moe_w3[0].astype(BF16), moe_w2[0].astype(BF16)

    x3s, hns, idxs, gates = [], [], [], []
    for x2, x2m, (bsz, seq) in zip(x2s, x2ms, shapes):
        x3, hn, idx, gate = _pool_router(x2, x2m, g_pool, w_pool_b, row(pool_scale[0]), g_moe, r_hi, r_lo,
                                         bsz, seq, TOKEN_TILE)
        x3s.append(x3); hns.append(hn); idxs.append(idx[:, :TOP_K]); gates.append(gate)

    seg_off, seg_slot, seg_nch, pos, tile_expert, tile_valid, n_tiles = _route(jnp.concatenate(idxs, axis=0))
    slots = jnp.zeros((n_tiles * EXPERT_TILE, D_MODEL), F32)
    tables, t0 = [], 0
    for hn in hns:
        t1 = t0 + hn.shape[0] // MOE_TILE
        seg = tuple(a[t0:t1].reshape(-1) for a in (seg_off, seg_slot, seg_nch))
        pos_g = pos[t0 * MOE_TILE:t1 * MOE_TILE]
        tables.append(seg + (pos_g,))
        slots = _dispatch(*seg, pos_g.T, hn, slots)
        t0 = t1
    y = _experts(tile_expert, tile_valid, slots, mw1, mw3, mw2)
    outs = []
    for x3, gate, tab, (bsz, seq) in zip(x3s, gates, tables, shapes):
        outs.append(_combine(*tab, x3, gate, y).reshape(bsz, seq, D_MODEL))
    return tuple(outs)


def kernel(x_prompt, x_sample, meta_tokens, norm_mix_e, w_in, q_gain, k_gain, rel_bias, conv_w, conv_b, conv_ln_g, conv_ln_b, w_out, norm_ffn_e, ffn_w1, ffn_w3, ffn_w2, norm_mix_o, pool_w, pool_scale, norm_ffn_o, router_w, moe_w1, moe_w3, moe_w2):
    return _trunk([x_prompt, x_sample], meta_tokens, norm_mix_e, w_in, q_gain, k_gain, rel_bias, conv_w,
                  conv_b, conv_ln_g, conv_ln_b, w_out, norm_ffn_e, ffn_w1, ffn_w3, ffn_w2,
                  norm_mix_o, pool_w, pool_scale, norm_ffn_o, router_w, moe_w1, moe_w3, moe_w2)
```

```python
import functools

import numpy as np
import jax
import jax.numpy as jnp
from jax import lax
from jax.experimental import pallas as pl
from jax.experimental.pallas import tpu as pltpu

D_MODEL = 1024
N_META = 16
GRID_W = 64
N_HEADS = 8
HEAD_DIM = 64
ATTN_DIM = N_HEADS * HEAD_DIM
CONV_DIM = D_MODEL - ATTN_DIM
IN_DIM = 3 * ATTN_DIM + 2 * CONV_DIM
WIN_ROWS = 8
WIN_COLS = 16
CONV_WIDTH = 31
CONV_HALF = CONV_WIDTH // 2
POOL_WINDOWS = (2, 4, 8, 16)
POOL_GROUP_DIM = D_MODEL // len(POOL_WINDOWS)
D_FF = 2816
N_EXPERTS = 8
TOP_K = 2
D_FF_EXPERT = 3584
EPS = 1e-6

F32 = jnp.float32
BF16 = jnp.bfloat16
SUBLANES = 8

TOKEN_TILE = 512
ATTN_ROWS = 4
ATTN_Q = ATTN_ROWS * GRID_W
ATTN_KROWS = ATTN_ROWS + WIN_ROWS
ATTN_KBLK = ATTN_KROWS * GRID_W // ATTN_Q
CONV_HALO = 16
CONV_CHUNK = 64
POOL_HALO = 8
EXPERT_TILE = 1024
EXPERT_FF_CHUNK = 512
MOE_TILE = 512
MOE_CHUNK = 8
MOE_ROWS = 1152
NEG_INF = -1e30
VMEM_LIMIT = 56 * 1024 * 1024


def _params(sem):
    return pltpu.CompilerParams(dimension_semantics=sem, vmem_limit_bytes=VMEM_LIMIT)


def _const_spec(shape):
    zeros = (0,) * len(shape)
    return pl.BlockSpec(shape, lambda *_: zeros)


def _row_spec(tm, cols):
    return pl.BlockSpec((tm, cols), lambda i: (i, 0))


def _rms(x, gain):
    ms = jnp.mean(x * x, axis=-1, keepdims=True)
    return x * lax.rsqrt(ms + EPS) * gain


def _proj_kernel(x_ref, g_ref, w_ref, qg_ref, kg_ref, hsum_ref, q_ref, k_ref, v_ref, c_ref):
    n = _rms(x_ref[...], g_ref[...]).astype(BF16)
    proj = jnp.dot(n, w_ref[...], preferred_element_type=F32)
    q = proj[:, :ATTN_DIM]
    k = proj[:, ATTN_DIM:2 * ATTN_DIM]
    v = proj[:, 2 * ATTN_DIM:3 * ATTN_DIM]
    u = proj[:, 3 * ATTN_DIM:3 * ATTN_DIM + CONV_DIM]
    g = proj[:, 3 * ATTN_DIM + CONV_DIM:]

    def head_norm(a, gain):
        ss = jnp.dot((a * a).astype(BF16), hsum_ref[...], preferred_element_type=F32)
        return a * lax.rsqrt(ss * (1.0 / HEAD_DIM) + EPS) * gain

    q_ref[...] = (head_norm(q, qg_ref[...]) * (HEAD_DIM ** -0.5)).astype(BF16)
    k_ref[...] = head_norm(k, kg_ref[...]).astype(BF16)
    v_ref[...] = v.astype(BF16)
    c_ref[...] = (u * jax.nn.sigmoid(g)).astype(BF16)


def _proj(x, norm_g, w_in, qg, kg, hsum, tm):
    n = x.shape[0]
    out = jax.ShapeDtypeStruct((n, ATTN_DIM), BF16)
    return pl.pallas_call(
        _proj_kernel,
        grid=(n // tm,),
        in_specs=[_row_spec(tm, D_MODEL), _const_spec((1, D_MODEL)), _const_spec((D_MODEL, IN_DIM)),
                  _const_spec((1, ATTN_DIM)), _const_spec((1, ATTN_DIM)), _const_spec((ATTN_DIM, ATTN_DIM))],
        out_specs=[_row_spec(tm, ATTN_DIM)] * 4,
        out_shape=[out] * 4,
        compiler_params=_params(("arbitrary",)),
        name="in_proj",
    )(x, norm_g, w_in, qg, kg, hsum)


def _attn_bias_patterns(rel_bias):
    n_dr, n_dc = 2 * WIN_ROWS - 1, 2 * WIN_COLS - 1
    qr = np.arange(ATTN_ROWS)[:, None]
    kr = np.arange(ATTN_KROWS)[None, :]
    qc = np.arange(GRID_W)[:, None]
    kc = np.arange(GRID_W)[None, :]
    cs = np.clip(qc - WIN_COLS // 2, 0, GRID_W - WIN_COLS)
    col_ok = (kc >= cs) & (kc < cs + WIN_COLS)
    dc = np.clip(kc - qc + (WIN_COLS - 1), 0, n_dc - 1)
    col_sel = (dc[None] == np.arange(n_dc)[:, None, None]).astype(np.float32)
    row_sel, ok = [], []
    for off, rs in ((0, 0 * qr), (ATTN_ROWS, qr), (2 * ATTN_ROWS, ATTN_ROWS + 0 * qr)):
        row_ok = (kr >= rs) & (kr < rs + WIN_ROWS)
        dr = np.clip(kr - (off + qr) + (WIN_ROWS - 1), 0, n_dr - 1)
        row_sel.append((dr[..., None] == np.arange(n_dr)).astype(np.float32))
        ok.append(row_ok[:, None, :, None] & col_ok[None, :, None, :])
    row_sel = np.stack(row_sel)
    ok = np.stack(ok).reshape(3, 1, ATTN_Q, ATTN_KROWS * GRID_W)
    vals = jnp.einsum('pqka,hab,bcd->phqckd', row_sel, rel_bias.astype(F32), col_sel,
                      precision=lax.Precision.HIGHEST)
    vals = vals.reshape(3, N_HEADS, ATTN_Q, ATTN_KROWS * GRID_W)
    return jnp.where(ok, vals, NEG_INF)


def _attn_kernel(q_ref, k0_ref, k1_ref, k2_ref, v0_ref, v1_ref, v2_ref, km_ref, vm_ref, bias_ref, o_ref):
    nt = (((1,), (1,)), ((), ()))
    k_refs = (k0_ref, k1_ref, k2_ref)
    v_refs = (v0_ref, v1_ref, v2_ref)
    outs = []
    for h in range(N_HEADS):
        sl = slice(h * HEAD_DIM, (h + 1) * HEAD_DIM)
        qh = q_ref[:, sl]
        s_meta = lax.dot_general(qh, km_ref[:, sl], nt, preferred_element_type=F32)
        s_loc = [lax.dot_general(qh, k_refs[t][:, sl], nt, preferred_element_type=F32)
                 + bias_ref[0, h, :, t * ATTN_Q:(t + 1) * ATTN_Q] for t in range(ATTN_KBLK)]
        m_loc = functools.reduce(jnp.maximum, s_loc)
        m = jnp.maximum(jnp.max(s_meta, axis=-1, keepdims=True), jnp.max(m_loc, axis=-1, keepdims=True))
        p_meta = jnp.exp(s_meta - m)
        acc = jnp.dot(p_meta.astype(BF16), vm_ref[:, sl], preferred_element_type=F32)
        p_sum = None
        for t in range(ATTN_KBLK):
            p = jnp.exp(s_loc[t] - m)
            p_sum = p if p_sum is None else p_sum + p
            acc = acc + jnp.dot(p.astype(BF16), v_refs[t][:, sl], preferred_element_type=F32)
        denom = jnp.sum(p_meta, axis=-1, keepdims=True) + jnp.sum(p_sum, axis=-1, keepdims=True)
        outs.append(acc / denom)
    o_ref[...] = jnp.concatenate(outs, axis=-1).astype(BF16)


def _attention(q, k, v, km, vm, bias_pat, bsz, seq):
    nb = seq // ATTN_Q
    assert seq % ATTN_Q == 0 and nb >= ATTN_KBLK

    def q_map(b, j):
        return (b * nb + j, 0)

    def kv_map(t):
        return lambda b, j: (b * nb + jnp.clip(j - 1, 0, nb - ATTN_KBLK) + t, 0)

    def bias_map(b, j):
        return (jnp.where(j == 0, 0, jnp.where(j == nb - 1, 2, 1)), 0, 0, 0)

    blk = (ATTN_Q, ATTN_DIM)
    kv_specs = [pl.BlockSpec(blk, kv_map(t)) for t in range(ATTN_KBLK)]
    return pl.pallas_call(
        _attn_kernel,
        grid=(bsz, nb),
        in_specs=[pl.BlockSpec(blk, q_map)] + kv_specs + kv_specs
                 + [_const_spec((N_META, ATTN_DIM)), _const_spec((N_META, ATTN_DIM)),
                    pl.BlockSpec((1, N_HEADS, ATTN_Q, ATTN_KBLK * ATTN_Q), bias_map)],
        out_specs=pl.BlockSpec(blk, q_map),
        out_shape=jax.ShapeDtypeStruct((bsz * seq, ATTN_DIM), BF16),
        compiler_params=_params(("arbitrary", "arbitrary")),
        name="nbr_attn",
    )(q, k, k, k, v, v, v, km, vm, bias_pat)


def _meta_attn_kernel(q_ref, k_ref, v_ref, o_ref):
    nt = (((1,), (1,)), ((), ()))
    outs = []
    for h in range(N_HEADS):
        sl = slice(h * HEAD_DIM, (h + 1) * HEAD_DIM)
        s = lax.dot_general(q_ref[:, sl], k_ref[:, sl], nt, preferred_element_type=F32)
        p = jnp.exp(s - jnp.max(s, axis=-1, keepdims=True))
        o = jnp.dot(p.astype(BF16), v_ref[:, sl], preferred_element_type=F32)
        outs.append(o / jnp.sum(p, axis=-1, keepdims=True))
    o_ref[...] = jnp.concatenate(outs, axis=-1).astype(BF16)


def _meta_attention(qm, km, vm):
    spec = _const_spec((N_META, ATTN_DIM))
    return pl.pallas_call(
        _meta_attn_kernel,
        in_specs=[spec] * 3, out_specs=spec, grid=(1,),
        out_shape=jax.ShapeDtypeStruct((N_META, ATTN_DIM), BF16),
        compiler_params=_params(("arbitrary",)),
        name="meta_attn",
    )(qm, km, vm)


def _conv_kernel(c_ref, prev_ref, next_ref, cm_ref, w_ref, b_ref, lg_ref, lb_ref,
                 out_ref, outm_ref, xs_ref, mpad_ref):
    tile = c_ref.shape[0]
    n_pad = tile + 2 * CONV_HALO
    i = pl.program_id(1)
    first = i == 0
    last = i == pl.num_programs(1) - 1
    cm = cm_ref[...].astype(F32)
    xs_ref[0, 0:CONV_HALO, :] = jnp.where(first, cm, prev_ref[...].astype(F32))
    xs_ref[0, CONV_HALO:CONV_HALO + tile, :] = c_ref[...].astype(F32)
    xs_ref[0, CONV_HALO + tile:, :] = jnp.where(last, 0.0, next_ref[...].astype(F32))
    for s in range(1, SUBLANES):
        xs_ref[s, 0:n_pad - SUBLANES, :] = xs_ref[0, s:s + n_pad - SUBLANES, :]

    def conv_rows(read, rows):
        acc = jnp.zeros((rows, CONV_DIM), F32)
        for j in range(CONV_WIDTH):
            acc = acc + w_ref[j:j + 1, :] * read(j - CONV_HALF)
        y = acc + b_ref[...]
        mu = jnp.mean(y, axis=-1, keepdims=True)
        yc = y - mu
        var = jnp.mean(yc * yc, axis=-1, keepdims=True)
        z = yc * lax.rsqrt(var + EPS) * lg_ref[...] + lb_ref[...]
        return (z * jax.nn.sigmoid(z)).astype(BF16)

    for rc in range(tile // CONV_CHUNK):
        start = CONV_HALO + rc * CONV_CHUNK

        def read(d, start=start):
            lo = start + d
            return xs_ref[lo % SUBLANES, lo - lo % SUBLANES:lo - lo % SUBLANES + CONV_CHUNK, :]

        out_ref[rc * CONV_CHUNK:(rc + 1) * CONV_CHUNK, :] = conv_rows(read, CONV_CHUNK)

    @pl.when(first)
    def _():
        mpad_ref[0:N_META, :] = jnp.zeros((N_META, CONV_DIM), F32)
        mpad_ref[N_META:2 * N_META, :] = cm
        mpad_ref[2 * N_META:, :] = c_ref[0:N_META, :].astype(F32)
        outm_ref[...] = conv_rows(lambda d: mpad_ref[N_META + d:2 * N_META + d, :], N_META)


def _conv(c, cm, conv_w, conv_b, ln_g, ln_b, bsz, seq, tile):
    tps = seq // tile
    hpt = tile // CONV_HALO
    n_halo = bsz * seq // CONV_HALO

    def main_map(b, i):
        return (b * tps + i, 0)

    def prev_map(b, i):
        return (jnp.maximum((b * tps + i) * hpt - 1, 0), 0)

    def next_map(b, i):
        return (jnp.minimum((b * tps + i + 1) * hpt, n_halo - 1), 0)

    halo = (CONV_HALO, CONV_DIM)
    vec = _const_spec((1, CONV_DIM))
    return pl.pallas_call(
        _conv_kernel,
        grid=(bsz, tps),
        in_specs=[pl.BlockSpec((tile, CONV_DIM), main_map), pl.BlockSpec(halo, prev_map),
                  pl.BlockSpec(halo, next_map), _const_spec((N_META, CONV_DIM)),
                  _const_spec((CONV_WIDTH + 1, CONV_DIM)), vec, vec, vec],
        out_specs=[pl.BlockSpec((tile, CONV_DIM), main_map),
                   pl.BlockSpec((N_META, CONV_DIM), lambda b, i: (b, 0))],
        out_shape=[jax.ShapeDtypeStruct((bsz * seq, CONV_DIM), BF16),
                   jax.ShapeDtypeStruct((bsz * N_META, CONV_DIM), BF16)],
        scratch_shapes=[pltpu.VMEM((SUBLANES, tile + 2 * CONV_HALO, CONV_DIM), F32),
                        pltpu.VMEM((3 * N_META, CONV_DIM), F32)],
        compiler_params=_params(("arbitrary", "arbitrary")),
        name="conv_ln_silu",
    )(c, c, c, cm, conv_w, conv_b, ln_g, ln_b)


def _out_proj_kernel(x_ref, a_ref, c_ref, wa_ref, wc_ref, o_ref):
    o_ref[...] = (x_ref[...]
                  + jnp.dot(a_ref[...], wa_ref[...], preferred_element_type=F32)
                  + jnp.dot(c_ref[...], wc_ref[...], preferred_element_type=F32))


def _out_proj(x, attn, cact, w_attn, w_conv, tm):
    n = x.shape[0]
    wspec = _const_spec((ATTN_DIM, D_MODEL))
    return pl.pallas_call(
        _out_proj_kernel,
        grid=(n // tm,),
        in_specs=[_row_spec(tm, D_MODEL), _row_spec(tm, ATTN_DIM), _row_spec(tm, CONV_DIM), wspec, wspec],
        out_specs=_row_spec(tm, D_MODEL),
        out_shape=jax.ShapeDtypeStruct((n, D_MODEL), F32),
        compiler_params=_params(("arbitrary",)),
        name="out_proj",
    )(x, attn, cact, w_attn, w_conv)


def _ffn_kernel(x_ref, g_ref, w1_ref, w3_ref, w2_ref, o_ref):
    x = x_ref[...]
    n = _rms(x, g_ref[...]).astype(BF16)
    h1 = jnp.dot(n, w1_ref[...], preferred_element_type=F32)
    h3 = jnp.dot(n, w3_ref[...], preferred_element_type=F32)
    act = (h1 * jax.nn.sigmoid(h1) * h3).astype(BF16)
    o_ref[...] = x + jnp.dot(act, w2_ref[...], preferred_element_type=F32)


def _ffn(x, g, w1, w3, w2, tm):
    n = x.shape[0]
    return pl.pallas_call(
        _ffn_kernel,
        grid=(n // tm,),
        in_specs=[_row_spec(tm, D_MODEL), _const_spec((1, D_MODEL)), _const_spec((D_MODEL, D_FF)),
                  _const_spec((D_MODEL, D_FF)), _const_spec((D_FF, D_MODEL))],
        out_specs=_row_spec(tm, D_MODEL),
        out_shape=jax.ShapeDtypeStruct((n, D_MODEL), F32),
        compiler_params=_params(("arbitrary",)),
        name="dense_swiglu",
    )(x, g, w1, w3, w2)


def _pool_router_kernel(x_ref, prev_ref, next_ref, xm_ref, g_ref, wp_ref, ps_ref, g2_ref, rhi_ref, rlo_ref,
                        x3_ref, hn_ref, gate_ref, pos_ref, pos_t_ref, cnt_ref, npad_ref, *, seq):
    tile = x_ref.shape[0]
    i = pl.program_id(1)
    first = i == 0
    last = i == pl.num_programs(1) - 1
    gain = g_ref[...]
    x = x_ref[...]
    n_main = _rms(x, gain)
    npad_ref[0:POOL_HALO, :] = _rms(jnp.where(first, xm_ref[...], prev_ref[...]), gain)
    npad_ref[POOL_HALO:POOL_HALO + tile, :] = n_main
    npad_ref[POOL_HALO + tile:, :] = jnp.where(last, 0.0, _rms(next_ref[...], gain))

    tok = i * tile + lax.broadcasted_iota(jnp.int32, (tile, 1), 0)
    mixed = []
    for gi, w in enumerate(POOL_WINDOWS):
        cols = slice(gi * POOL_GROUP_DIM, (gi + 1) * POOL_GROUP_DIM)
        half = w // 2
        acc = jnp.zeros((tile, POOL_GROUP_DIM), F32)
        for d in range(-half, half):
            acc = acc + npad_ref[POOL_HALO + d:POOL_HALO + d + tile, cols]
        count = (w - jnp.maximum(tok + half - seq, 0)).astype(F32)
        diff = (acc / count - n_main[:, cols]).astype(BF16)
        mixed.append(jnp.dot(diff, wp_ref[gi], preferred_element_type=F32))
    x3 = x + jnp.concatenate(mixed, axis=-1) * ps_ref[...]
    x3_ref[...] = x3

    hn = _rms(x3, g2_ref[...])
    hn_ref[...] = hn
    h_hi = hn.astype(BF16)
    h_lo = (hn - h_hi.astype(F32)).astype(BF16)
    logits = (jnp.dot(h_hi, rhi_ref[...], preferred_element_type=F32)
              + jnp.dot(h_lo, rhi_ref[...], preferred_element_type=F32)
              + jnp.dot(h_hi, rlo_ref[...], preferred_element_type=F32))
    lane = lax.broadcasted_iota(jnp.int32, logits.shape, 1)
    lane_f = lane.astype(F32)
    logits = jnp.where(lane < N_EXPERTS, logits, NEG_INF)
    m1 = jnp.max(logits, axis=-1, keepdims=True)
    i1 = jnp.min(jnp.where(logits == m1, lane_f, 256.0), axis=-1, keepdims=True)
    rest = jnp.where(lane_f == i1, NEG_INF, logits)
    m2 = jnp.max(rest, axis=-1, keepdims=True)
    i2 = jnp.min(jnp.where(rest == m2, lane_f, 256.0), axis=-1, keepdims=True)
    e2 = jnp.exp(m2 - m1)
    g1 = 1.0 / (1.0 + e2)
    gate_ref[...] = jnp.where(lane == 0, g1, jnp.where(lane == 1, e2 * g1, 0.0))

    hot = [jnp.where(lane_f == i1, 1.0, 0.0), jnp.where(lane_f == i2, 1.0, 0.0)]
    earlier = jnp.where(lax.broadcasted_iota(jnp.int32, (tile, tile), 1)
                        < lax.broadcasted_iota(jnp.int32, (tile, tile), 0), 1.0, 0.0).astype(BF16)
    before = [jnp.dot(earlier, h.astype(BF16), preferred_element_type=F32) for h in hot]
    count = [jnp.sum(h, axis=0, keepdims=True) for h in hot]
    total = count[0] + count[1]
    chunks = jnp.floor((total + (MOE_CHUNK - 1)) * (1.0 / MOE_CHUNK))
    lower_expert = jnp.where(lax.broadcasted_iota(jnp.int32, (128, 128), 0)
                             < lax.broadcasted_iota(jnp.int32, (128, 128), 1), 1.0, 0.0).astype(BF16)
    seg_off = MOE_CHUNK * jnp.dot(jnp.broadcast_to(chunks, (SUBLANES, 128)).astype(BF16), lower_expert,
                                  preferred_element_type=F32)[0:1, :]
    pos0 = jnp.sum(hot[0] * (before[0] + seg_off), axis=-1, keepdims=True)
    pos1 = jnp.sum(hot[1] * (before[1] + count[0] + seg_off), axis=-1, keepdims=True)
    pos = jnp.where(lane == 0, pos0, jnp.where(lane == 1, pos1, 0.0))
    pos_ref[...] = pos.astype(jnp.int32)
    pos_t_ref[...] = jnp.transpose(pos)[0:SUBLANES, :].astype(jnp.int32)
    cnt_ref[...] = jnp.broadcast_to(total, (SUBLANES, 128)).astype(jnp.int32)


def _pool_router(x2, x2m, g, w_pool, pool_scale, g2, r_hi, r_lo, bsz, seq, tile):
    tps = seq // tile
    hpt = tile // POOL_HALO
    n_halo = bsz * seq // POOL_HALO

    def main_map(b, i):
        return (b * tps + i, 0)

    def prev_map(b, i):
        return (jnp.maximum((b * tps + i) * hpt - 1, 0), 0)

    def next_map(b, i):
        return (jnp.minimum((b * tps + i + 1) * hpt, n_halo - 1), 0)

    halo = (POOL_HALO, D_MODEL)
    vec = _const_spec((1, D_MODEL))
    main = pl.BlockSpec((tile, D_MODEL), main_map)
    small = pl.BlockSpec((tile, 128), main_map)
    n = bsz * seq
    return pl.pallas_call(
        functools.partial(_pool_router_kernel, seq=seq),
        grid=(bsz, tps),
        in_specs=[main, pl.BlockSpec(halo, prev_map), pl.BlockSpec(halo, next_map),
                  pl.BlockSpec(halo, lambda b, i: (2 * b + 1, 0)), vec,
                  _const_spec((len(POOL_WINDOWS), POOL_GROUP_DIM, POOL_GROUP_DIM)), vec, vec,
                  _const_spec((D_MODEL, 128)), _const_spec((D_MODEL, 128))],
        out_specs=[main, main, small, small,
                   pl.BlockSpec((SUBLANES, tile), lambda b, i: (0, b * tps + i)),
                   pl.BlockSpec((None, SUBLANES, 128), lambda b, i: (b * tps + i, 0, 0))],
        out_shape=[jax.ShapeDtypeStruct((n, D_MODEL), F32), jax.ShapeDtypeStruct((n, D_MODEL), F32),
                   jax.ShapeDtypeStruct((n, 128), F32), jax.ShapeDtypeStruct((n, 128), jnp.int32),
                   jax.ShapeDtypeStruct((SUBLANES, n), jnp.int32),
                   jax.ShapeDtypeStruct((n // tile, SUBLANES, 128), jnp.int32)],
        scratch_shapes=[pltpu.VMEM((tile + 2 * POOL_HALO, D_MODEL), F32)],
        compiler_params=_params(("arbitrary", "arbitrary")),
        name="pool_router",
    )(x2, x2, x2, x2m, g, w_pool, pool_scale, g2, r_hi, r_lo)


def _start_chunks(copy, first_a, first_b, n_chunks):
    def start(c, carry):
        copy(pl.multiple_of(first_a + c * MOE_CHUNK, MOE_CHUNK),
             pl.multiple_of(first_b + c * MOE_CHUNK, MOE_CHUNK)).start()
        return carry

    lax.fori_loop(0, n_chunks, start, 0)


def _wait_chunks(copy, n_chunks):
    def wait(c, carry):
        copy(0, 0).wait()
        return carry

    lax.fori_loop(0, n_chunks, wait, 0)


def _start_segments(copy, off_ref, slot_ref, nch_ref, t):
    for e in range(N_EXPERTS):
        _start_chunks(copy, off_ref[t * N_EXPERTS + e], slot_ref[t * N_EXPERTS + e], nch_ref[t * N_EXPERTS + e])


def _wait_segments(copy, nch_ref, t):
    _wait_chunks(copy, sum(nch_ref[t * N_EXPERTS + e] for e in range(N_EXPERTS)))


def _dispatch_kernel(off_ref, slot_ref, nch_ref, tail_ref, ntail_ref, pos_ref, *refs, group_tiles):
    n_groups = len(group_tiles) - 1
    hn_refs = refs[:n_groups]
    xs_ref, buf_ref, zero_ref, sem = refs[n_groups:]
    t = pl.program_id(0)
    rows = lax.broadcasted_iota(jnp.int32, (MOE_ROWS, MOE_TILE), 0)
    perm = jnp.where(rows == pos_ref[0:1, :], 1.0, jnp.where(rows == pos_ref[1:2, :], 1.0, 0.0)).astype(BF16)
    for g in range(n_groups):
        @pl.when((t >= group_tiles[g]) & (t < group_tiles[g + 1]))
        def _(g=g):
            buf_ref[...] = jnp.dot(perm, hn_refs[g][...].astype(BF16), preferred_element_type=F32)

    def copy(buf_row, slot_row):
        return pltpu.make_async_copy(buf_ref.at[pl.ds(buf_row, MOE_CHUNK)],
                                     xs_ref.at[pl.ds(slot_row, MOE_CHUNK)], sem)

    _start_segments(copy, off_ref, slot_ref, nch_ref, t)
    _wait_segments(copy, nch_ref, t)

    @pl.when(t == pl.num_programs(0) - 1)
    def _():
        zero_ref[...] = jnp.zeros(zero_ref.shape, F32)

        def fill(unused_row, slot_row):
            del unused_row
            return pltpu.make_async_copy(zero_ref, xs_ref.at[pl.ds(slot_row, MOE_CHUNK)], sem)

        for e in range(N_EXPERTS):
            _start_chunks(fill, 0, tail_ref[e], ntail_ref[e])
        _wait_chunks(fill, sum(ntail_ref[e] for e in range(N_EXPERTS)))


def _dispatch(seg_tables, tail_tables, pos_t, hns, n_slots):
    group_tiles = [0]
    for hn in hns:
        group_tiles.append(group_tiles[-1] + hn.shape[0] // MOE_TILE)

    def hn_spec(g):
        first, count = group_tiles[g], group_tiles[g + 1] - group_tiles[g]
        return pl.BlockSpec((MOE_TILE, D_MODEL), lambda i, *_: (jnp.clip(i - first, 0, count - 1), 0))

    grid_spec = pltpu.PrefetchScalarGridSpec(
        num_scalar_prefetch=5,
        grid=(group_tiles[-1],),
        in_specs=[pl.BlockSpec((SUBLANES, MOE_TILE), lambda i, *_: (0, i))] + [hn_spec(g) for g in range(len(hns))],
        out_specs=pl.BlockSpec(memory_space=pl.ANY),
        scratch_shapes=[pltpu.VMEM((MOE_ROWS, D_MODEL), F32), pltpu.VMEM((MOE_CHUNK, D_MODEL), F32),
                        pltpu.SemaphoreType.DMA],
    )
    return pl.pallas_call(
        functools.partial(_dispatch_kernel, group_tiles=tuple(group_tiles)),
        grid_spec=grid_spec,
        out_shape=jax.ShapeDtypeStruct((n_slots, D_MODEL), F32),
        compiler_params=_params(("arbitrary",)),
        name="moe_dispatch",
    )(*seg_tables, *tail_tables, pos_t, *hns)


def _expert_kernel(te_ref, tv_ref, x_ref, w1_ref, w3_ref, w2_ref, y_ref, xb_ref, act_ref):
    del te_ref
    j = pl.program_id(0)
    c = pl.program_id(1)
    n_chunks = act_ref.shape[0]
    valid = tv_ref[j] == 1
    last = c == n_chunks - 1

    @pl.when(valid & (c == 0))
    def _():
        xb_ref[...] = x_ref[...].astype(BF16)

    @pl.when(valid)
    def _():
        xb = xb_ref[...]
        h1 = jnp.dot(xb, w1_ref[...], preferred_element_type=F32)
        h3 = jnp.dot(xb, w3_ref[...], preferred_element_type=F32)
        act_ref[c] = (h1 * jax.nn.sigmoid(h1) * h3).astype(BF16)

    @pl.when(valid & last)
    def _():
        acc = jnp.dot(act_ref[0], w2_ref[0:EXPERT_FF_CHUNK, :], preferred_element_type=F32)
        for cc in range(1, n_chunks):
            acc = acc + jnp.dot(act_ref[cc], w2_ref[cc * EXPERT_FF_CHUNK:(cc + 1) * EXPERT_FF_CHUNK, :],
                                preferred_element_type=F32)
        y_ref[...] = acc

    @pl.when(jnp.logical_not(valid) & last)
    def _():
        y_ref[...] = jnp.zeros(y_ref.shape, F32)


def _experts(tile_expert, tile_valid, xs, w1, w3, w2):
    n_slots = xs.shape[0]
    n_tiles = n_slots // EXPERT_TILE
    n_chunks = D_FF_EXPERT // EXPERT_FF_CHUNK

    def chunk(j, c, tv):
        return jnp.where(tv[j] == 1, c, n_chunks - 1)

    grid_spec = pltpu.PrefetchScalarGridSpec(
        num_scalar_prefetch=2,
        grid=(n_tiles, n_chunks),
        in_specs=[pl.BlockSpec((EXPERT_TILE, D_MODEL), lambda j, c, te, tv: (jnp.where(tv[j] == 1, j, 0), 0)),
                  pl.BlockSpec((None, D_MODEL, EXPERT_FF_CHUNK), lambda j, c, te, tv: (te[j], 0, chunk(j, c, tv))),
                  pl.BlockSpec((None, D_MODEL, EXPERT_FF_CHUNK), lambda j, c, te, tv: (te[j], 0, chunk(j, c, tv))),
                  pl.BlockSpec((None, D_FF_EXPERT, D_MODEL), lambda j, c, te, tv: (te[j], 0, 0),
                               pipeline_mode=pl.Buffered(1))],
        out_specs=pl.BlockSpec((EXPERT_TILE, D_MODEL), lambda j, c, te, tv: (j, 0)),
        scratch_shapes=[pltpu.VMEM((EXPERT_TILE, D_MODEL), BF16),
                        pltpu.VMEM((n_chunks, EXPERT_TILE, EXPERT_FF_CHUNK), BF16)],
    )
    return pl.pallas_call(
        _expert_kernel,
        grid_spec=grid_spec,
        out_shape=jax.ShapeDtypeStruct((n_slots, D_MODEL), F32),
        compiler_params=_params(("arbitrary", "arbitrary")),
        name="moe_experts",
    )(tile_expert, tile_valid, xs, w1, w3, w2)


def _combine_kernel(off_ref, slot_ref, nch_ref, pos_ref, x_ref, gate_ref, y_ref, o_ref, buf_ref, sem):
    t = pl.program_id(0)
    cur = lax.rem(t, 2)

    def copy_into(b):
        def copy(buf_row, slot_row):
            return pltpu.make_async_copy(y_ref.at[pl.ds(slot_row, MOE_CHUNK)],
                                         buf_ref.at[b, pl.ds(buf_row, MOE_CHUNK)], sem.at[b])
        return copy

    @pl.when(t == 0)
    def _():
        buf_ref[...] = jnp.zeros(buf_ref.shape, F32)
        _start_segments(copy_into(0), off_ref, slot_ref, nch_ref, 0)

    @pl.when(t + 1 < pl.num_programs(0))
    def _():
        _start_segments(copy_into(1 - cur), off_ref, slot_ref, nch_ref, t + 1)

    _wait_segments(copy_into(cur), nch_ref, t)
    yb = buf_ref[cur].astype(BF16)
    cols = lax.broadcasted_iota(jnp.int32, (MOE_TILE, MOE_ROWS), 1)
    pos = pos_ref[...]
    gates = gate_ref[...]
    out = x_ref[...]
    for k in range(TOP_K):
        pick = jnp.where(cols == pos[:, k:k + 1], 1.0, 0.0).astype(BF16)
        out = out + gates[:, k:k + 1] * jnp.dot(pick, yb, preferred_element_type=F32)
    o_ref[...] = out


def _combine(seg_tables, pos, x3, gates, y):
    n = x3.shape[0]
    grid_spec = pltpu.PrefetchScalarGridSpec(
        num_scalar_prefetch=3,
        grid=(n // MOE_TILE,),
        in_specs=[pl.BlockSpec((MOE_TILE, 128), lambda i, *_: (i, 0)),
                  pl.BlockSpec((MOE_TILE, D_MODEL), lambda i, *_: (i, 0)),
                  pl.BlockSpec((MOE_TILE, 128), lambda i, *_: (i, 0)),
                  pl.BlockSpec(memory_space=pl.ANY)],
        out_specs=pl.BlockSpec((MOE_TILE, D_MODEL), lambda i, *_: (i, 0)),
        scratch_shapes=[pltpu.VMEM((2, MOE_ROWS, D_MODEL), F32), pltpu.SemaphoreType.DMA((2,))],
    )
    return pl.pallas_call(
        _combine_kernel,
        grid_spec=grid_spec,
        out_shape=jax.ShapeDtypeStruct((n, D_MODEL), F32),
        compiler_params=_params(("arbitrary",)),
        name="moe_combine",
    )(*seg_tables, pos, x3, gates, y)


def _route(counts):
    n_tok_tiles = counts.shape[0]
    seg = (counts + MOE_CHUNK - 1) // MOE_CHUNK * MOE_CHUNK
    seg_off = jnp.cumsum(seg, axis=1) - seg
    used = jnp.sum(seg, axis=0)
    padded = (used + EXPERT_TILE - 1) // EXPERT_TILE * EXPERT_TILE
    pad_end = jnp.cumsum(padded)
    pad_start = pad_end - padded
    seg_slot = pad_start[None, :] + jnp.cumsum(seg, axis=0) - seg
    worst_rows = n_tok_tiles * (TOP_K * MOE_TILE + N_EXPERTS * (MOE_CHUNK - 1)) + N_EXPERTS * (EXPERT_TILE - 1)
    n_tiles = (worst_rows + EXPERT_TILE - 1) // EXPERT_TILE
    tile_start = jnp.arange(n_tiles, dtype=jnp.int32) * EXPERT_TILE
    tile_expert = jnp.sum((tile_start[:, None] >= pad_end[None, :]).astype(jnp.int32), axis=1)
    tile_expert = jnp.minimum(tile_expert, N_EXPERTS - 1)
    tile_valid = (tile_start < pad_end[-1]).astype(jnp.int32)
    segs = (seg_off, seg_slot, seg // MOE_CHUNK)
    tail_end = pad_end.at[N_EXPERTS - 1].set(n_tiles * EXPERT_TILE)
    tails = (pad_start + used, (tail_end - pad_start - used) // MOE_CHUNK)
    return segs, tails, tile_expert, tile_valid, n_tiles


def _trunk(groups, meta_tokens, norm_mix_e, w_in, q_gain, k_gain, rel_bias, conv_w, conv_b,
           conv_ln_g, conv_ln_b, w_out, norm_ffn_e, ffn_w1, ffn_w3, ffn_w2,
           norm_mix_o, pool_w, pool_scale, norm_ffn_o, router_w, moe_w1, moe_w3, moe_w2):
    row = lambda a: a.reshape(1, -1).astype(F32)
    shapes = [(g.shape[0], g.shape[1]) for g in groups]
    xs = [g.reshape(-1, D_MODEL) for g in groups]

    hsum = jnp.asarray(np.kron(np.eye(N_HEADS), np.ones((HEAD_DIM, HEAD_DIM))), BF16)
    qg = row(jnp.tile(q_gain[0], N_HEADS))
    kg = row(jnp.tile(k_gain[0], N_HEADS))
    w_in_b = w_in[0].astype(BF16)
    g_mix = row(norm_mix_e[0])
    bias_pat = _attn_bias_patterns(rel_bias[0])
    conv_w_p = jnp.concatenate([conv_w[0].astype(F32), jnp.zeros((1, CONV_DIM), F32)], axis=0)
    w_out_a = w_out[0, :ATTN_DIM].astype(BF16)
    w_out_c = w_out[0, ATTN_DIM:].astype(BF16)
    g_ffn = row(norm_ffn_e[0])
    w1b, w3b, w2b = ffn_w1[0].astype(BF16), ffn_w3[0].astype(BF16), ffn_w2[0].astype(BF16)

    meta = meta_tokens.astype(F32)
    qm, km, vm, cm = _proj(meta, g_mix, w_in_b, qg, kg, hsum, N_META)
    attn_m = _meta_attention(qm, km, vm)

    x2s, x2ms = [], []
    for x, (bsz, seq) in zip(xs, shapes):
        q, k, v, c = _proj(x, g_mix, w_in_b, qg, kg, hsum, TOKEN_TILE)
        attn = _attention(q, k, v, km, vm, bias_pat, bsz, seq)
        cact, cact_m = _conv(c, cm, conv_w_p, row(conv_b[0]), row(conv_ln_g[0]), row(conv_ln_b[0]),
                             bsz, seq, TOKEN_TILE)
        x1 = _out_proj(x, attn, cact, w_out_a, w_out_c, TOKEN_TILE)
        x1m = _out_proj(jnp.tile(meta, (bsz, 1)), jnp.tile(attn_m, (bsz, 1)), cact_m, w_out_a, w_out_c,
                        bsz * N_META)
        x2s.append(_ffn(x1, g_ffn, w1b, w3b, w2b, TOKEN_TILE))
        x2ms.append(_ffn(x1m, g_ffn, w1b, w3b, w2b, bsz * N_META))

    g_pool = row(norm_mix_o[0])
    w_pool_b = pool_w[0].astype(BF16)
    g_moe = row(norm_ffn_o[0])
    rw = jnp.zeros((D_MODEL, 128), F32).at[:, :N_EXPERTS].set(router_w[0].astype(F32))
    r_hi = rw.astype(BF16)
    r_lo = (rw - r_hi.astype(F32)).astype(BF16)
    mw1, mw3, mw2 = moe_w1[0].astype(BF16), moe_w3[0].astype(BF16), moe_w2[0].astype(BF16)

    assert TOKEN_TILE == MOE_TILE
    routed = [_pool_router(x2, x2m, g_pool, w_pool_b, row(pool_scale[0]), g_moe, r_hi, r_lo, bsz, seq, MOE_TILE)
              for x2, x2m, (bsz, seq) in zip(x2s, x2ms, shapes)]
    counts = jnp.concatenate([r[5][:, 0, :N_EXPERTS] for r in routed], axis=0)
    segs, tails, tile_expert, tile_valid, n_tiles = _route(counts)
    slots = _dispatch(tuple(a.reshape(-1) for a in segs), tails,
                      jnp.concatenate([r[4] for r in routed], axis=1), [r[1] for r in routed],
                      n_tiles * EXPERT_TILE)
    y = _experts(tile_expert, tile_valid, slots, mw1, mw3, mw2)
    outs, t0 = [], 0
    for (x3, _, gate, pos, _, cnt), (bsz, seq) in zip(routed, shapes):
        t1 = t0 + cnt.shape[0]
        seg_tables = tuple(a[t0:t1].reshape(-1) for a in segs)
        outs.append(_combine(seg_tables, pos, x3, gate, y).reshape(bsz, seq, D_MODEL))
        t0 = t1
    return tuple(outs)


def kernel(x_prompt, x_sample, meta_tokens, norm_mix_e, w_in, q_gain, k_gain, rel_bias, conv_w, conv_b, conv_ln_g, conv_ln_b, w_out, norm_ffn_e, ffn_w1, ffn_w3, ffn_w2, norm_mix_o, pool_w, pool_scale, norm_ffn_o, router_w, moe_w1, moe_w3, moe_w2):
    return _trunk([x_prompt, x_sample], meta_tokens, norm_mix_e, w_in, q_gain, k_gain, rel_bias, conv_w,
                  conv_b, conv_ln_g, conv_ln_b, w_out, norm_ffn_e, ffn_w1, ffn_w3, ffn_w2,
                  norm_mix_o, pool_w, pool_scale, norm_ffn_o, router_w, moe_w1, moe_w3, moe_w2)
```

```python
import functools

import numpy as np
import jax
import jax.numpy as jnp
from jax import lax
from jax.experimental import pallas as pl
from jax.experimental.pallas import tpu as pltpu

D_MODEL = 1024
N_META = 16
GRID_W = 64
N_HEADS = 8
HEAD_DIM = 64
ATTN_DIM = N_HEADS * HEAD_DIM
CONV_DIM = D_MODEL - ATTN_DIM
IN_DIM = 3 * ATTN_DIM + 2 * CONV_DIM
WIN_ROWS = 8
WIN_COLS = 16
CONV_WIDTH = 31
CONV_HALF = CONV_WIDTH // 2
POOL_WINDOWS = (2, 4, 8, 16)
POOL_GROUP_DIM = D_MODEL // len(POOL_WINDOWS)
D_FF = 2816
N_EXPERTS = 8
TOP_K = 2
D_FF_EXPERT = 3584
EPS = 1e-6

F32 = jnp.float32
BF16 = jnp.bfloat16
SUBLANES = 8

TOKEN_TILE = 512
ATTN_ROWS = 4
ATTN_Q = ATTN_ROWS * GRID_W
ATTN_KROWS = ATTN_ROWS + WIN_ROWS
ATTN_KBLK = ATTN_KROWS * GRID_W // ATTN_Q
CONV_HALO = 16
CONV_CHUNK = 64
POOL_HALO = 8
EXPERT_TILE = 1024
EXPERT_FF_CHUNK = 512
MOE_TILE = 512
MOE_CHUNK = 8
MOE_BIG = 32
MOE_ROWS = 1152
NEG_INF = -1e30
VMEM_LIMIT = 56 * 1024 * 1024


def _params(sem):
    return pltpu.CompilerParams(dimension_semantics=sem, vmem_limit_bytes=VMEM_LIMIT)


def _const_spec(shape):
    zeros = (0,) * len(shape)
    return pl.BlockSpec(shape, lambda *_: zeros)


def _row_spec(tm, cols):
    return pl.BlockSpec((tm, cols), lambda i: (i, 0))


def _rms(x, gain):
    ms = jnp.mean(x * x, axis=-1, keepdims=True)
    return x * lax.rsqrt(ms + EPS) * gain


def _proj_kernel(x_ref, g_ref, w_ref, qg_ref, kg_ref, hsum_ref, q_ref, k_ref, v_ref, c_ref):
    n = _rms(x_ref[...], g_ref[...]).astype(BF16)
    proj = jnp.dot(n, w_ref[...], preferred_element_type=F32)
    q = proj[:, :ATTN_DIM]
    k = proj[:, ATTN_DIM:2 * ATTN_DIM]
    v = proj[:, 2 * ATTN_DIM:3 * ATTN_DIM]
    u = proj[:, 3 * ATTN_DIM:3 * ATTN_DIM + CONV_DIM]
    g = proj[:, 3 * ATTN_DIM + CONV_DIM:]

    def head_norm(a, gain):
        ss = jnp.dot((a * a).astype(BF16), hsum_ref[...], preferred_element_type=F32)
        return a * lax.rsqrt(ss * (1.0 / HEAD_DIM) + EPS) * gain

    q_ref[...] = (head_norm(q, qg_ref[...]) * (HEAD_DIM ** -0.5)).astype(BF16)
    k_ref[...] = head_norm(k, kg_ref[...]).astype(BF16)
    v_ref[...] = v.astype(BF16)
    c_ref[...] = (u * jax.nn.sigmoid(g)).astype(BF16)


def _proj(x, norm_g, w_in, qg, kg, hsum, tm):
    n = x.shape[0]
    out = jax.ShapeDtypeStruct((n, ATTN_DIM), BF16)
    return pl.pallas_call(
        _proj_kernel,
        grid=(n // tm,),
        in_specs=[_row_spec(tm, D_MODEL), _const_spec((1, D_MODEL)), _const_spec((D_MODEL, IN_DIM)),
                  _const_spec((1, ATTN_DIM)), _const_spec((1, ATTN_DIM)), _const_spec((ATTN_DIM, ATTN_DIM))],
        out_specs=[_row_spec(tm, ATTN_DIM)] * 4,
        out_shape=[out] * 4,
        compiler_params=_params(("arbitrary",)),
        name="in_proj",
    )(x, norm_g, w_in, qg, kg, hsum)


def _attn_bias_patterns(rel_bias):
    n_dr, n_dc = 2 * WIN_ROWS - 1, 2 * WIN_COLS - 1
    qr = np.arange(ATTN_ROWS)[:, None]
    kr = np.arange(ATTN_KROWS)[None, :]
    qc = np.arange(GRID_W)[:, None]
    kc = np.arange(GRID_W)[None, :]
    cs = np.clip(qc - WIN_COLS // 2, 0, GRID_W - WIN_COLS)
    col_ok = (kc >= cs) & (kc < cs + WIN_COLS)
    dc = np.clip(kc - qc + (WIN_COLS - 1), 0, n_dc - 1)
    col_sel = (dc[None] == np.arange(n_dc)[:, None, None]).astype(np.float32)
    row_sel, ok = [], []
    for off, rs in ((0, 0 * qr), (ATTN_ROWS, qr), (2 * ATTN_ROWS, ATTN_ROWS + 0 * qr)):
        row_ok = (kr >= rs) & (kr < rs + WIN_ROWS)
        dr = np.clip(kr - (off + qr) + (WIN_ROWS - 1), 0, n_dr - 1)
        row_sel.append((dr[..., None] == np.arange(n_dr)).astype(np.float32))
        ok.append(row_ok[:, None, :, None] & col_ok[None, :, None, :])
    row_sel = np.stack(row_sel)
    ok = np.stack(ok).reshape(3, 1, ATTN_Q, ATTN_KROWS * GRID_W)
    vals = jnp.einsum('pqka,hab,bcd->phqckd', row_sel, rel_bias.astype(F32), col_sel,
                      precision=lax.Precision.HIGHEST)
    vals = vals.reshape(3, N_HEADS, ATTN_Q, ATTN_KROWS * GRID_W)
    return jnp.where(ok, vals, NEG_INF)


def _attn_kernel(q_ref, k0_ref, k1_ref, k2_ref, v0_ref, v1_ref, v2_ref, km_ref, vm_ref, bias_ref, o_ref):
    nt = (((1,), (1,)), ((), ()))
    k_refs = (k0_ref, k1_ref, k2_ref)
    v_refs = (v0_ref, v1_ref, v2_ref)
    outs = []
    for h in range(N_HEADS):
        sl = slice(h * HEAD_DIM, (h + 1) * HEAD_DIM)
        qh = q_ref[:, sl]
        s_meta = lax.dot_general(qh, km_ref[:, sl], nt, preferred_element_type=F32)
        s_loc = [lax.dot_general(qh, k_refs[t][:, sl], nt, preferred_element_type=F32)
                 + bias_ref[0, h, :, t * ATTN_Q:(t + 1) * ATTN_Q] for t in range(ATTN_KBLK)]
        m_loc = functools.reduce(jnp.maximum, s_loc)
        m = jnp.maximum(jnp.max(s_meta, axis=-1, keepdims=True), jnp.max(m_loc, axis=-1, keepdims=True))
        p_meta = jnp.exp(s_meta - m)
        acc = jnp.dot(p_meta.astype(BF16), vm_ref[:, sl], preferred_element_type=F32)
        p_sum = None
        for t in range(ATTN_KBLK):
            p = jnp.exp(s_loc[t] - m)
            p_sum = p if p_sum is None else p_sum + p
            acc = acc + jnp.dot(p.astype(BF16), v_refs[t][:, sl], preferred_element_type=F32)
        denom = jnp.sum(p_meta, axis=-1, keepdims=True) + jnp.sum(p_sum, axis=-1, keepdims=True)
        outs.append(acc / denom)
    o_ref[...] = jnp.concatenate(outs, axis=-1).astype(BF16)


def _attention(q, k, v, km, vm, bias_pat, bsz, seq):
    nb = seq // ATTN_Q
    assert seq % ATTN_Q == 0 and nb >= ATTN_KBLK

    def q_map(b, j):
        return (b * nb + j, 0)

    def kv_map(t):
        return lambda b, j: (b * nb + jnp.clip(j - 1, 0, nb - ATTN_KBLK) + t, 0)

    def bias_map(b, j):
        return (jnp.where(j == 0, 0, jnp.where(j == nb - 1, 2, 1)), 0, 0, 0)

    blk = (ATTN_Q, ATTN_DIM)
    kv_specs = [pl.BlockSpec(blk, kv_map(t)) for t in range(ATTN_KBLK)]
    return pl.pallas_call(
        _attn_kernel,
        grid=(bsz, nb),
        in_specs=[pl.BlockSpec(blk, q_map)] + kv_specs + kv_specs
                 + [_const_spec((N_META, ATTN_DIM)), _const_spec((N_META, ATTN_DIM)),
                    pl.BlockSpec((1, N_HEADS, ATTN_Q, ATTN_KBLK * ATTN_Q), bias_map)],
        out_specs=pl.BlockSpec(blk, q_map),
        out_shape=jax.ShapeDtypeStruct((bsz * seq, ATTN_DIM), BF16),
        compiler_params=_params(("arbitrary", "arbitrary")),
        name="nbr_attn",
    )(q, k, k, k, v, v, v, km, vm, bias_pat)


def _meta_attn_kernel(q_ref, k_ref, v_ref, o_ref):
    nt = (((1,), (1,)), ((), ()))
    outs = []
    for h in range(N_HEADS):
        sl = slice(h * HEAD_DIM, (h + 1) * HEAD_DIM)
        s = lax.dot_general(q_ref[:, sl], k_ref[:, sl], nt, preferred_element_type=F32)
        p = jnp.exp(s - jnp.max(s, axis=-1, keepdims=True))
        o = jnp.dot(p.astype(BF16), v_ref[:, sl], preferred_element_type=F32)
        outs.append(o / jnp.sum(p, axis=-1, keepdims=True))
    o_ref[...] = jnp.concatenate(outs, axis=-1).astype(BF16)


def _meta_attention(qm, km, vm):
    spec = _const_spec((N_META, ATTN_DIM))
    return pl.pallas_call(
        _meta_attn_kernel,
        in_specs=[spec] * 3, out_specs=spec, grid=(1,),
        out_shape=jax.ShapeDtypeStruct((N_META, ATTN_DIM), BF16),
        compiler_params=_params(("arbitrary",)),
        name="meta_attn",
    )(qm, km, vm)


def _conv_kernel(c_ref, prev_ref, next_ref, cm_ref, w_ref, b_ref, lg_ref, lb_ref,
                 out_ref, outm_ref, xs_ref, mpad_ref):
    tile = c_ref.shape[0]
    n_pad = tile + 2 * CONV_HALO
    i = pl.program_id(1)
    first = i == 0
    last = i == pl.num_programs(1) - 1
    cm = cm_ref[...].astype(F32)
    xs_ref[0, 0:CONV_HALO, :] = jnp.where(first, cm, prev_ref[...].astype(F32))
    xs_ref[0, CONV_HALO:CONV_HALO + tile, :] = c_ref[...].astype(F32)
    xs_ref[0, CONV_HALO + tile:, :] = jnp.where(last, 0.0, next_ref[...].astype(F32))
    for s in range(1, SUBLANES):
        xs_ref[s, 0:n_pad - SUBLANES, :] = xs_ref[0, s:s + n_pad - SUBLANES, :]

    def conv_rows(read, rows):
        acc = jnp.zeros((rows, CONV_DIM), F32)
        for j in range(CONV_WIDTH):
            acc = acc + w_ref[j:j + 1, :] * read(j - CONV_HALF)
        y = acc + b_ref[...]
        mu = jnp.mean(y, axis=-1, keepdims=True)
        yc = y - mu
        var = jnp.mean(yc * yc, axis=-1, keepdims=True)
        z = yc * lax.rsqrt(var + EPS) * lg_ref[...] + lb_ref[...]
        return (z * jax.nn.sigmoid(z)).astype(BF16)

    for rc in range(tile // CONV_CHUNK):
        start = CONV_HALO + rc * CONV_CHUNK

        def read(d, start=start):
            lo = start + d
            return xs_ref[lo % SUBLANES, lo - lo % SUBLANES:lo - lo % SUBLANES + CONV_CHUNK, :]

        out_ref[rc * CONV_CHUNK:(rc + 1) * CONV_CHUNK, :] = conv_rows(read, CONV_CHUNK)

    @pl.when(first)
    def _():
        mpad_ref[0:N_META, :] = jnp.zeros((N_META, CONV_DIM), F32)
        mpad_ref[N_META:2 * N_META, :] = cm
        mpad_ref[2 * N_META:, :] = c_ref[0:N_META, :].astype(F32)
        outm_ref[...] = conv_rows(lambda d: mpad_ref[N_META + d:2 * N_META + d, :], N_META)


def _conv(c, cm, conv_w, conv_b, ln_g, ln_b, bsz, seq, tile):
    tps = seq // tile
    hpt = tile // CONV_HALO
    n_halo = bsz * seq // CONV_HALO

    def main_map(b, i):
        return (b * tps + i, 0)

    def prev_map(b, i):
        return (jnp.maximum((b * tps + i) * hpt - 1, 0), 0)

    def next_map(b, i):
        return (jnp.minimum((b * tps + i + 1) * hpt, n_halo - 1), 0)

    halo = (CONV_HALO, CONV_DIM)
    vec = _const_spec((1, CONV_DIM))
    return pl.pallas_call(
        _conv_kernel,
        grid=(bsz, tps),
        in_specs=[pl.BlockSpec((tile, CONV_DIM), main_map), pl.BlockSpec(halo, prev_map),
                  pl.BlockSpec(halo, next_map), _const_spec((N_META, CONV_DIM)),
                  _const_spec((CONV_WIDTH + 1, CONV_DIM)), vec, vec, vec],
        out_specs=[pl.BlockSpec((tile, CONV_DIM), main_map),
                   pl.BlockSpec((N_META, CONV_DIM), lambda b, i: (b, 0))],
        out_shape=[jax.ShapeDtypeStruct((bsz * seq, CONV_DIM), BF16),
                   jax.ShapeDtypeStruct((bsz * N_META, CONV_DIM), BF16)],
        scratch_shapes=[pltpu.VMEM((SUBLANES, tile + 2 * CONV_HALO, CONV_DIM), F32),
                        pltpu.VMEM((3 * N_META, CONV_DIM), F32)],
        compiler_params=_params(("arbitrary", "arbitrary")),
        name="conv_ln_silu",
    )(c, c, c, cm, conv_w, conv_b, ln_g, ln_b)


def _out_ffn_kernel(x_ref, a_ref, c_ref, wa_ref, wc_ref, g_ref, w1_ref, w3_ref, w2_ref, o_ref):
    x = (x_ref[...]
         + jnp.dot(a_ref[...], wa_ref[...], preferred_element_type=F32)
         + jnp.dot(c_ref[...], wc_ref[...], preferred_element_type=F32))
    n = _rms(x, g_ref[...]).astype(BF16)
    h1 = jnp.dot(n, w1_ref[...], preferred_element_type=F32)
    h3 = jnp.dot(n, w3_ref[...], preferred_element_type=F32)
    act = (h1 * jax.nn.sigmoid(h1) * h3).astype(BF16)
    o_ref[...] = x + jnp.dot(act, w2_ref[...], preferred_element_type=F32)


def _out_ffn(x, attn, cact, w_attn, w_conv, g, w1, w3, w2, tm):
    n = x.shape[0]
    wspec = _const_spec((ATTN_DIM, D_MODEL))
    return pl.pallas_call(
        _out_ffn_kernel,
        grid=(n // tm,),
        in_specs=[_row_spec(tm, D_MODEL), _row_spec(tm, ATTN_DIM), _row_spec(tm, CONV_DIM), wspec, wspec,
                  _const_spec((1, D_MODEL)), _const_spec((D_MODEL, D_FF)), _const_spec((D_MODEL, D_FF)),
                  _const_spec((D_FF, D_MODEL))],
        out_specs=_row_spec(tm, D_MODEL),
        out_shape=jax.ShapeDtypeStruct((n, D_MODEL), F32),
        compiler_params=_params(("arbitrary",)),
        name="out_proj_swiglu",
    )(x, attn, cact, w_attn, w_conv, g, w1, w3, w2)


def _pool_router_kernel(x_ref, prev_ref, next_ref, xm_ref, g_ref, wp_ref, ps_ref, g2_ref, rhi_ref, rlo_ref,
                        x3_ref, hn_ref, gate_ref, pos_ref, pos_t_ref, cnt_ref, npad_ref, *, seq):
    tile = x_ref.shape[0]
    i = pl.program_id(1)
    first = i == 0
    last = i == pl.num_programs(1) - 1
    gain = g_ref[...]
    x = x_ref[...]
    n_main = _rms(x, gain)
    npad_ref[0:POOL_HALO, :] = _rms(jnp.where(first, xm_ref[...], prev_ref[...]), gain)
    npad_ref[POOL_HALO:POOL_HALO + tile, :] = n_main
    npad_ref[POOL_HALO + tile:, :] = jnp.where(last, 0.0, _rms(next_ref[...], gain))

    tok = i * tile + lax.broadcasted_iota(jnp.int32, (tile, 1), 0)
    mixed = []
    for gi, w in enumerate(POOL_WINDOWS):
        cols = slice(gi * POOL_GROUP_DIM, (gi + 1) * POOL_GROUP_DIM)
        half = w // 2
        acc = jnp.zeros((tile, POOL_GROUP_DIM), F32)
        for d in range(-half, half):
            acc = acc + npad_ref[POOL_HALO + d:POOL_HALO + d + tile, cols]
        count = (w - jnp.maximum(tok + half - seq, 0)).astype(F32)
        diff = (acc / count - n_main[:, cols]).astype(BF16)
        mixed.append(jnp.dot(diff, wp_ref[gi], preferred_element_type=F32))
    x3 = x + jnp.concatenate(mixed, axis=-1) * ps_ref[...]
    x3_ref[...] = x3

    hn = _rms(x3, g2_ref[...])
    hn_ref[...] = hn
    h_hi = hn.astype(BF16)
    h_lo = (hn - h_hi.astype(F32)).astype(BF16)
    logits = (jnp.dot(h_hi, rhi_ref[...], preferred_element_type=F32)
              + jnp.dot(h_lo, rhi_ref[...], preferred_element_type=F32)
              + jnp.dot(h_hi, rlo_ref[...], preferred_element_type=F32))
    lane = lax.broadcasted_iota(jnp.int32, logits.shape, 1)
    lane_f = lane.astype(F32)
    logits = jnp.where(lane < N_EXPERTS, logits, NEG_INF)
    m1 = jnp.max(logits, axis=-1, keepdims=True)
    i1 = jnp.min(jnp.where(logits == m1, lane_f, 256.0), axis=-1, keepdims=True)
    rest = jnp.where(lane_f == i1, NEG_INF, logits)
    m2 = jnp.max(rest, axis=-1, keepdims=True)
    i2 = jnp.min(jnp.where(rest == m2, lane_f, 256.0), axis=-1, keepdims=True)
    e2 = jnp.exp(m2 - m1)
    g1 = 1.0 / (1.0 + e2)
    gate_ref[...] = jnp.where(lane == 0, g1, jnp.where(lane == 1, e2 * g1, 0.0))

    hot = [jnp.where(lane_f == i1, 1.0, 0.0), jnp.where(lane_f == i2, 1.0, 0.0)]
    earlier = jnp.where(lax.broadcasted_iota(jnp.int32, (tile, tile), 1)
                        < lax.broadcasted_iota(jnp.int32, (tile, tile), 0), 1.0, 0.0).astype(BF16)
    before = [jnp.dot(earlier, h.astype(BF16), preferred_element_type=F32) for h in hot]
    count = [jnp.sum(h, axis=0, keepdims=True) for h in hot]
    total = count[0] + count[1]
    chunks = jnp.floor((total + (MOE_CHUNK - 1)) * (1.0 / MOE_CHUNK))
    lower_expert = jnp.where(lax.broadcasted_iota(jnp.int32, (128, 128), 0)
                             < lax.broadcasted_iota(jnp.int32, (128, 128), 1), 1.0, 0.0).astype(BF16)
    seg_off = MOE_CHUNK * jnp.dot(jnp.broadcast_to(chunks, (SUBLANES, 128)).astype(BF16), lower_expert,
                                  preferred_element_type=F32)[0:1, :]
    pos0 = jnp.sum(hot[0] * (before[0] + seg_off), axis=-1, keepdims=True)
    pos1 = jnp.sum(hot[1] * (before[1] + count[0] + seg_off), axis=-1, keepdims=True)
    pos = jnp.where(lane == 0, pos0, jnp.where(lane == 1, pos1, 0.0))
    pos_ref[...] = pos.astype(jnp.int32)
    pos_t_ref[...] = jnp.transpose(pos)[0:SUBLANES, :].astype(jnp.int32)
    cnt_ref[...] = jnp.broadcast_to(total, (SUBLANES, 128)).astype(jnp.int32)


def _pool_router(x2, x2m, g, w_pool, pool_scale, g2, r_hi, r_lo, bsz, seq, tile):
    tps = seq // tile
    hpt = tile // POOL_HALO
    n_halo = bsz * seq // POOL_HALO

    def main_map(b, i):
        return (b * tps + i, 0)

    def prev_map(b, i):
        return (jnp.maximum((b * tps + i) * hpt - 1, 0), 0)

    def next_map(b, i):
        return (jnp.minimum((b * tps + i + 1) * hpt, n_halo - 1), 0)

    halo = (POOL_HALO, D_MODEL)
    vec = _const_spec((1, D_MODEL))
    main = pl.BlockSpec((tile, D_MODEL), main_map)
    small = pl.BlockSpec((tile, 128), main_map)
    n = bsz * seq
    return pl.pallas_call(
        functools.partial(_pool_router_kernel, seq=seq),
        grid=(bsz, tps),
        in_specs=[main, pl.BlockSpec(halo, prev_map), pl.BlockSpec(halo, next_map),
                  pl.BlockSpec(halo, lambda b, i: (2 * b + 1, 0)), vec,
                  _const_spec((len(POOL_WINDOWS), POOL_GROUP_DIM, POOL_GROUP_DIM)), vec, vec,
                  _const_spec((D_MODEL, 128)), _const_spec((D_MODEL, 128))],
        out_specs=[main, main, small, small,
                   pl.BlockSpec((SUBLANES, tile), lambda b, i: (0, b * tps + i)),
                   pl.BlockSpec((None, SUBLANES, 128), lambda b, i: (b * tps + i, 0, 0))],
        out_shape=[jax.ShapeDtypeStruct((n, D_MODEL), F32), jax.ShapeDtypeStruct((n, D_MODEL), F32),
                   jax.ShapeDtypeStruct((n, 128), F32), jax.ShapeDtypeStruct((n, 128), jnp.int32),
                   jax.ShapeDtypeStruct((SUBLANES, n), jnp.int32),
                   jax.ShapeDtypeStruct((n // tile, SUBLANES, 128), jnp.int32)],
        scratch_shapes=[pltpu.VMEM((tile + 2 * POOL_HALO, D_MODEL), F32)],
        compiler_params=_params(("arbitrary", "arbitrary")),
        name="pool_router",
    )(x2, x2, x2, x2m, g, w_pool, pool_scale, g2, r_hi, r_lo)


def _split_chunks(n_chunks):
    per_big = MOE_BIG // MOE_CHUNK
    n_big = lax.div(n_chunks, per_big)
    return n_big, n_chunks - n_big * per_big


def _start_pieces(copy, first_a, first_b, n_chunks):
    n_big, n_small = _split_chunks(n_chunks)

    def start(rows, base_a, base_b):
        def body(c, carry):
            copy(pl.multiple_of(base_a + c * rows, MOE_CHUNK), pl.multiple_of(base_b + c * rows, MOE_CHUNK),
                 rows).start()
            return carry
        return body

    lax.fori_loop(0, n_big, start(MOE_BIG, first_a, first_b), 0)
    lax.fori_loop(0, n_small, start(MOE_CHUNK, first_a + n_big * MOE_BIG, first_b + n_big * MOE_BIG), 0)


def _wait_pieces(copy, chunk_counts):
    splits = [_split_chunks(n) for n in chunk_counts]
    for rows, count in ((MOE_BIG, sum(s[0] for s in splits)), (MOE_CHUNK, sum(s[1] for s in splits))):
        def wait(c, carry, rows=rows):
            copy(0, 0, rows).wait()
            return carry

        lax.fori_loop(0, count, wait, 0)


def _start_segments(copy, off_ref, slot_ref, nch_ref, t):
    for e in range(N_EXPERTS):
        _start_pieces(copy, off_ref[t * N_EXPERTS + e], slot_ref[t * N_EXPERTS + e], nch_ref[t * N_EXPERTS + e])


def _wait_segments(copy, nch_ref, t):
    _wait_pieces(copy, [nch_ref[t * N_EXPERTS + e] for e in range(N_EXPERTS)])


def _dispatch_kernel(off_ref, slot_ref, nch_ref, tail_ref, ntail_ref, pos_ref, *refs, group_tiles):
    n_groups = len(group_tiles) - 1
    hn_refs = refs[:n_groups]
    xs_ref, buf_ref, zero_ref, sem = refs[n_groups:]
    t = pl.program_id(0)
    cur = lax.rem(t, 2)

    def copy_from(b):
        def copy(buf_row, slot_row, rows):
            return pltpu.make_async_copy(buf_ref.at[b, pl.ds(buf_row, rows)], xs_ref.at[pl.ds(slot_row, rows)],
                                         sem.at[b])
        return copy

    @pl.when(t >= 2)
    def _():
        _wait_segments(copy_from(cur), nch_ref, t - 2)

    rows = lax.broadcasted_iota(jnp.int32, (MOE_ROWS, MOE_TILE), 0)
    perm = jnp.where(rows == pos_ref[0:1, :], 1.0, jnp.where(rows == pos_ref[1:2, :], 1.0, 0.0)).astype(BF16)
    for g in range(n_groups):
        @pl.when((t >= group_tiles[g]) & (t < group_tiles[g + 1]))
        def _(g=g):
            buf_ref[cur] = jnp.dot(perm, hn_refs[g][...].astype(BF16), preferred_element_type=F32)

    _start_segments(copy_from(cur), off_ref, slot_ref, nch_ref, t)

    @pl.when(t == pl.num_programs(0) - 1)
    def _():
        @pl.when(t >= 1)
        def _():
            _wait_segments(copy_from(1 - cur), nch_ref, t - 1)

        _wait_segments(copy_from(cur), nch_ref, t)

        zero_ref[...] = jnp.zeros(zero_ref.shape, F32)

        def fill(unused_row, slot_row, rows):
            del unused_row
            return pltpu.make_async_copy(zero_ref.at[pl.ds(0, rows)], xs_ref.at[pl.ds(slot_row, rows)], sem.at[0])

        for e in range(N_EXPERTS):
            _start_pieces(fill, 0, tail_ref[e], ntail_ref[e])
        _wait_pieces(fill, [ntail_ref[e] for e in range(N_EXPERTS)])


def _dispatch(seg_tables, tail_tables, pos_t, hns, n_slots):
    group_tiles = [0]
    for hn in hns:
        group_tiles.append(group_tiles[-1] + hn.shape[0] // MOE_TILE)

    def hn_spec(g):
        first, count = group_tiles[g], group_tiles[g + 1] - group_tiles[g]
        return pl.BlockSpec((MOE_TILE, D_MODEL), lambda i, *_: (jnp.clip(i - first, 0, count - 1), 0))

    grid_spec = pltpu.PrefetchScalarGridSpec(
        num_scalar_prefetch=5,
        grid=(group_tiles[-1],),
        in_specs=[pl.BlockSpec((SUBLANES, MOE_TILE), lambda i, *_: (0, i))] + [hn_spec(g) for g in range(len(hns))],
        out_specs=pl.BlockSpec(memory_space=pl.ANY),
        scratch_shapes=[pltpu.VMEM((2, MOE_ROWS, D_MODEL), F32), pltpu.VMEM((MOE_BIG, D_MODEL), F32),
                        pltpu.SemaphoreType.DMA((2,))],
    )
    return pl.pallas_call(
        functools.partial(_dispatch_kernel, group_tiles=tuple(group_tiles)),
        grid_spec=grid_spec,
        out_shape=jax.ShapeDtypeStruct((n_slots, D_MODEL), F32),
        compiler_params=_params(("arbitrary",)),
        name="moe_dispatch",
    )(*seg_tables, *tail_tables, pos_t, *hns)


def _expert_kernel(te_ref, tv_ref, x_ref, w1_ref, w3_ref, w2_ref, y_ref, xb_ref, act_ref):
    del te_ref
    j = pl.program_id(0)
    c = pl.program_id(1)
    n_chunks = act_ref.shape[0]
    valid = tv_ref[j] == 1
    last = c == n_chunks - 1

    @pl.when(valid & (c == 0))
    def _():
        xb_ref[...] = x_ref[...].astype(BF16)

    @pl.when(valid)
    def _():
        xb = xb_ref[...]
        h1 = jnp.dot(xb, w1_ref[...], preferred_element_type=F32)
        h3 = jnp.dot(xb, w3_ref[...], preferred_element_type=F32)
        act_ref[c] = (h1 * jax.nn.sigmoid(h1) * h3).astype(BF16)

    @pl.when(valid & last)
    def _():
        acc = jnp.dot(act_ref[0], w2_ref[0:EXPERT_FF_CHUNK, :], preferred_element_type=F32)
        for cc in range(1, n_chunks):
            acc = acc + jnp.dot(act_ref[cc], w2_ref[cc * EXPERT_FF_CHUNK:(cc + 1) * EXPERT_FF_CHUNK, :],
                                preferred_element_type=F32)
        y_ref[...] = acc

    @pl.when(jnp.logical_not(valid) & last)
    def _():
        y_ref[...] = jnp.zeros(y_ref.shape, F32)


def _experts(tile_expert, tile_valid, xs, w1, w3, w2):
    n_slots = xs.shape[0]
    n_tiles = n_slots // EXPERT_TILE
    n_chunks = D_FF_EXPERT // EXPERT_FF_CHUNK

    def chunk(j, c, tv):
        return jnp.where(tv[j] == 1, c, n_chunks - 1)

    grid_spec = pltpu.PrefetchScalarGridSpec(
        num_scalar_prefetch=2,
        grid=(n_tiles, n_chunks),
        in_specs=[pl.BlockSpec((EXPERT_TILE, D_MODEL), lambda j, c, te, tv: (jnp.where(tv[j] == 1, j, 0), 0)),
                  pl.BlockSpec((None, D_MODEL, EXPERT_FF_CHUNK), lambda j, c, te, tv: (te[j], 0, chunk(j, c, tv))),
                  pl.BlockSpec((None, D_MODEL, EXPERT_FF_CHUNK), lambda j, c, te, tv: (te[j], 0, chunk(j, c, tv))),
                  pl.BlockSpec((None, D_FF_EXPERT, D_MODEL), lambda j, c, te, tv: (te[j], 0, 0),
                               pipeline_mode=pl.Buffered(1))],
        out_specs=pl.BlockSpec((EXPERT_TILE, D_MODEL), lambda j, c, te, tv: (j, 0)),
        scratch_shapes=[pltpu.VMEM((EXPERT_TILE, D_MODEL), BF16),
                        pltpu.VMEM((n_chunks, EXPERT_TILE, EXPERT_FF_CHUNK), BF16)],
    )
    return pl.pallas_call(
        _expert_kernel,
        grid_spec=grid_spec,
        out_shape=jax.ShapeDtypeStruct((n_slots, D_MODEL), F32),
        compiler_params=_params(("arbitrary", "arbitrary")),
        name="moe_experts",
    )(tile_expert, tile_valid, xs, w1, w3, w2)


def _combine_kernel(off_ref, slot_ref, nch_ref, pos_ref, x_ref, gate_ref, y_ref, o_ref, buf_ref, sem):
    t = pl.program_id(0)
    cur = lax.rem(t, 2)

    def copy_into(b):
        def copy(buf_row, slot_row, rows):
            return pltpu.make_async_copy(y_ref.at[pl.ds(slot_row, rows)], buf_ref.at[b, pl.ds(buf_row, rows)],
                                         sem.at[b])
        return copy

    @pl.when(t == 0)
    def _():
        buf_ref[...] = jnp.zeros(buf_ref.shape, F32)
        _start_segments(copy_into(0), off_ref, slot_ref, nch_ref, 0)

    @pl.when(t + 1 < pl.num_programs(0))
    def _():
        _start_segments(copy_into(1 - cur), off_ref, slot_ref, nch_ref, t + 1)

    _wait_segments(copy_into(cur), nch_ref, t)
    yb = buf_ref[cur].astype(BF16)
    cols = lax.broadcasted_iota(jnp.int32, (MOE_TILE, MOE_ROWS), 1)
    pos = pos_ref[...]
    gates = gate_ref[...]
    out = x_ref[...]
    for k in range(TOP_K):
        pick = jnp.where(cols == pos[:, k:k + 1], 1.0, 0.0).astype(BF16)
        out = out + gates[:, k:k + 1] * jnp.dot(pick, yb, preferred_element_type=F32)
    o_ref[...] = out


def _combine(seg_tables, pos, x3, gates, y):
    n = x3.shape[0]
    grid_spec = pltpu.PrefetchScalarGridSpec(
        num_scalar_prefetch=3,
        grid=(n // MOE_TILE,),
        in_specs=[pl.BlockSpec((MOE_TILE, 128), lambda i, *_: (i, 0)),
                  pl.BlockSpec((MOE_TILE, D_MODEL), lambda i, *_: (i, 0)),
                  pl.BlockSpec((MOE_TILE, 128), lambda i, *_: (i, 0)),
                  pl.BlockSpec(memory_space=pl.ANY)],
        out_specs=pl.BlockSpec((MOE_TILE, D_MODEL), lambda i, *_: (i, 0)),
        scratch_shapes=[pltpu.VMEM((2, MOE_ROWS, D_MODEL), F32), pltpu.SemaphoreType.DMA((2,))],
    )
    return pl.pallas_call(
        _combine_kernel,
        grid_spec=grid_spec,
        out_shape=jax.ShapeDtypeStruct((n, D_MODEL), F32),
        compiler_params=_params(("arbitrary",)),
        name="moe_combine",
    )(*seg_tables, pos, x3, gates, y)


def _route(counts):
    n_tok_tiles = counts.shape[0]
    seg = (counts + MOE_CHUNK - 1) // MOE_CHUNK * MOE_CHUNK
    seg_off = jnp.cumsum(seg, axis=1) - seg
    used = jnp.sum(seg, axis=0)
    padded = (used + EXPERT_TILE - 1) // EXPERT_TILE * EXPERT_TILE
    pad_end = jnp.cumsum(padded)
    pad_start = pad_end - padded
    seg_slot = pad_start[None, :] + jnp.cumsum(seg, axis=0) - seg
    worst_rows = n_tok_tiles * (TOP_K * MOE_TILE + N_EXPERTS * (MOE_CHUNK - 1)) + N_EXPERTS * (EXPERT_TILE - 1)
    n_tiles = (worst_rows + EXPERT_TILE - 1) // EXPERT_TILE
    tile_start = jnp.arange(n_tiles, dtype=jnp.int32) * EXPERT_TILE
    tile_expert = jnp.sum((tile_start[:, None] >= pad_end[None, :]).astype(jnp.int32), axis=1)
    tile_expert = jnp.minimum(tile_expert, N_EXPERTS - 1)
    tile_valid = (tile_start < pad_end[-1]).astype(jnp.int32)
    segs = (seg_off, seg_slot, seg // MOE_CHUNK)
    tail_end = pad_end.at[N_EXPERTS - 1].set(n_tiles * EXPERT_TILE)
    tails = (pad_start + used, (tail_end - pad_start - used) // MOE_CHUNK)
    return segs, tails, tile_expert, tile_valid, n_tiles


def _trunk(groups, meta_tokens, norm_mix_e, w_in, q_gain, k_gain, rel_bias, conv_w, conv_b,
           conv_ln_g, conv_ln_b, w_out, norm_ffn_e, ffn_w1, ffn_w3, ffn_w2,
           norm_mix_o, pool_w, pool_scale, norm_ffn_o, router_w, moe_w1, moe_w3, moe_w2):
    row = lambda a: a.reshape(1, -1).astype(F32)
    shapes = [(g.shape[0], g.shape[1]) for g in groups]
    xs = [g.reshape(-1, D_MODEL) for g in groups]

    hsum = jnp.asarray(np.kron(np.eye(N_HEADS), np.ones((HEAD_DIM, HEAD_DIM))), BF16)
    qg = row(jnp.tile(q_gain[0], N_HEADS))
    kg = row(jnp.tile(k_gain[0], N_HEADS))
    w_in_b = w_in[0].astype(BF16)
    g_mix = row(norm_mix_e[0])
    bias_pat = _attn_bias_patterns(rel_bias[0])
    conv_w_p = jnp.concatenate([conv_w[0].astype(F32), jnp.zeros((1, CONV_DIM), F32)], axis=0)
    w_out_a = w_out[0, :ATTN_DIM].astype(BF16)
    w_out_c = w_out[0, ATTN_DIM:].astype(BF16)
    g_ffn = row(norm_ffn_e[0])
    w1b, w3b, w2b = ffn_w1[0].astype(BF16), ffn_w3[0].astype(BF16), ffn_w2[0].astype(BF16)

    meta = meta_tokens.astype(F32)
    qm, km, vm, cm = _proj(meta, g_mix, w_in_b, qg, kg, hsum, N_META)
    attn_m = _meta_attention(qm, km, vm)

    x2s, x2ms = [], []
    for x, (bsz, seq) in zip(xs, shapes):
        q, k, v, c = _proj(x, g_mix, w_in_b, qg, kg, hsum, TOKEN_TILE)
        attn = _attention(q, k, v, km, vm, bias_pat, bsz, seq)
        cact, cact_m = _conv(c, cm, conv_w_p, row(conv_b[0]), row(conv_ln_g[0]), row(conv_ln_b[0]),
                             bsz, seq, TOKEN_TILE)
        x2s.append(_out_ffn(x, attn, cact, w_out_a, w_out_c, g_ffn, w1b, w3b, w2b, TOKEN_TILE))
        x2ms.append(_out_ffn(jnp.tile(meta, (bsz, 1)), jnp.tile(attn_m, (bsz, 1)), cact_m, w_out_a, w_out_c,
                             g_ffn, w1b, w3b, w2b, bsz * N_META))

    g_pool = row(norm_mix_o[0])
    w_pool_b = pool_w[0].astype(BF16)
    g_moe = row(norm_ffn_o[0])
    rw = jnp.zeros((D_MODEL, 128), F32).at[:, :N_EXPERTS].set(router_w[0].astype(F32))
    r_hi = rw.astype(BF16)
    r_lo = (rw - r_hi.astype(F32)).astype(BF16)
    mw1, mw3, mw2 = moe_w1[0].astype(BF16), moe_w3[0].astype(BF16), moe_w2[0].astype(BF16)

    assert TOKEN_TILE == MOE_TILE
    routed = [_pool_router(x2, x2m, g_pool, w_pool_b, row(pool_scale[0]), g_moe, r_hi, r_lo, bsz, seq, MOE_TILE)
              for x2, x2m, (bsz, seq) in zip(x2s, x2ms, shapes)]
    counts = jnp.concatenate([r[5][:, 0, :N_EXPERTS] for r in routed], axis=0)
    segs, tails, tile_expert, tile_valid, n_tiles = _route(counts)
    slots = _dispatch(tuple(a.reshape(-1) for a in segs), tails,
                      jnp.concatenate([r[4] for r in routed], axis=1), [r[1] for r in routed],
                      n_tiles * EXPERT_TILE)
    y = _experts(tile_expert, tile_valid, slots, mw1, mw3, mw2)
    outs, t0 = [], 0
    for (x3, _, gate, pos, _, cnt), (bsz, seq) in zip(routed, shapes):
        t1 = t0 + cnt.shape[0]
        seg_tables = tuple(a[t0:t1].reshape(-1) for a in segs)
        outs.append(_combine(seg_tables, pos, x3, gate, y).reshape(bsz, seq, D_MODEL))
        t0 = t1
    return tuple(outs)


def kernel(x_prompt, x_sample, meta_tokens, norm_mix_e, w_in, q_gain, k_gain, rel_bias, conv_w, conv_b, conv_ln_g, conv_ln_b, w_out, norm_ffn_e, ffn_w1, ffn_w3, ffn_w2, norm_mix_o, pool_w, pool_scale, norm_ffn_o, router_w, moe_w1, moe_w3, moe_w2):
    return _trunk([x_prompt, x_sample], meta_tokens, norm_mix_e, w_in, q_gain, k_gain, rel_bias, conv_w,
                  conv_b, conv_ln_g, conv_ln_b, w_out, norm_ffn_e, ffn_w1, ffn_w3, ffn_w2,
                  norm_mix_o, pool_w, pool_scale, norm_ffn_o, router_w, moe_w1, moe_w3, moe_w2)
```

```python
import functools

import numpy as np
import jax
import jax.numpy as jnp
from jax import lax
from jax.experimental import pallas as pl
from jax.experimental.pallas import tpu as pltpu

D_MODEL = 1024
N_META = 16
GRID_W = 64
N_HEADS = 8
HEAD_DIM = 64
ATTN_DIM = N_HEADS * HEAD_DIM
CONV_DIM = D_MODEL - ATTN_DIM
IN_DIM = 3 * ATTN_DIM + 2 * CONV_DIM
WIN_ROWS = 8
WIN_COLS = 16
CONV_WIDTH = 31
CONV_HALF = CONV_WIDTH // 2
POOL_WINDOWS = (2, 4, 8, 16)
POOL_GROUP_DIM = D_MODEL // len(POOL_WINDOWS)
D_FF = 2816
N_EXPERTS = 8
TOP_K = 2
D_FF_EXPERT = 3584
EPS = 1e-6

F32 = jnp.float32
BF16 = jnp.bfloat16
SUBLANES = 8

TOKEN_TILE = 512
ATTN_ROWS = 4
ATTN_Q = ATTN_ROWS * GRID_W
ATTN_KROWS = ATTN_ROWS + WIN_ROWS
ATTN_KBLK = ATTN_KROWS * GRID_W // ATTN_Q
CONV_HALO = 16
CONV_CHUNK = 64
POOL_HALO = 8
EXPERT_TILE = 1024
EXPERT_FF_CHUNK = 512
MOE_TILE = 512
MOE_CHUNK = 8
MOE_BIG = 32
MOE_ROWS = 1152
NEG_INF = -1e30
VMEM_LIMIT = 56 * 1024 * 1024


def _params(sem):
    return pltpu.CompilerParams(dimension_semantics=sem, vmem_limit_bytes=VMEM_LIMIT)


def _const_spec(shape):
    zeros = (0,) * len(shape)
    return pl.BlockSpec(shape, lambda *_: zeros)


def _row_spec(tm, cols):
    return pl.BlockSpec((tm, cols), lambda i: (i, 0))


def _rms(x, gain):
    ms = jnp.mean(x * x, axis=-1, keepdims=True)
    return x * lax.rsqrt(ms + EPS) * gain


def _proj_kernel(x_ref, g_ref, w_ref, qg_ref, kg_ref, hsum_ref, q_ref, k_ref, v_ref, c_ref):
    n = _rms(x_ref[...], g_ref[...]).astype(BF16)
    proj = jnp.dot(n, w_ref[...], preferred_element_type=F32)
    q = proj[:, :ATTN_DIM]
    k = proj[:, ATTN_DIM:2 * ATTN_DIM]
    v = proj[:, 2 * ATTN_DIM:3 * ATTN_DIM]
    u = proj[:, 3 * ATTN_DIM:3 * ATTN_DIM + CONV_DIM]
    g = proj[:, 3 * ATTN_DIM + CONV_DIM:]

    def head_norm(a, gain):
        ss = jnp.dot((a * a).astype(BF16), hsum_ref[...], preferred_element_type=F32)
        return a * lax.rsqrt(ss * (1.0 / HEAD_DIM) + EPS) * gain

    q_ref[...] = (head_norm(q, qg_ref[...]) * (HEAD_DIM ** -0.5)).astype(BF16)
    k_ref[...] = head_norm(k, kg_ref[...]).astype(BF16)
    v_ref[...] = v.astype(BF16)
    c_ref[...] = (u * jax.nn.sigmoid(g)).astype(BF16)


def _proj(x, norm_g, w_in, qg, kg, hsum, tm):
    n = x.shape[0]
    out = jax.ShapeDtypeStruct((n, ATTN_DIM), BF16)
    return pl.pallas_call(
        _proj_kernel,
        grid=(n // tm,),
        in_specs=[_row_spec(tm, D_MODEL), _const_spec((1, D_MODEL)), _const_spec((D_MODEL, IN_DIM)),
                  _const_spec((1, ATTN_DIM)), _const_spec((1, ATTN_DIM)), _const_spec((ATTN_DIM, ATTN_DIM))],
        out_specs=[_row_spec(tm, ATTN_DIM)] * 4,
        out_shape=[out] * 4,
        compiler_params=_params(("arbitrary",)),
        name="in_proj",
    )(x, norm_g, w_in, qg, kg, hsum)


def _attn_bias_patterns(rel_bias):
    n_dr, n_dc = 2 * WIN_ROWS - 1, 2 * WIN_COLS - 1
    qr = np.arange(ATTN_ROWS)[:, None]
    kr = np.arange(ATTN_KROWS)[None, :]
    qc = np.arange(GRID_W)[:, None]
    kc = np.arange(GRID_W)[None, :]
    cs = np.clip(qc - WIN_COLS // 2, 0, GRID_W - WIN_COLS)
    col_ok = (kc >= cs) & (kc < cs + WIN_COLS)
    dc = np.clip(kc - qc + (WIN_COLS - 1), 0, n_dc - 1)
    col_sel = (dc[None] == np.arange(n_dc)[:, None, None]).astype(np.float32)
    row_sel, ok = [], []
    for off, rs in ((0, 0 * qr), (ATTN_ROWS, qr), (2 * ATTN_ROWS, ATTN_ROWS + 0 * qr)):
        row_ok = (kr >= rs) & (kr < rs + WIN_ROWS)
        dr = np.clip(kr - (off + qr) + (WIN_ROWS - 1), 0, n_dr - 1)
        row_sel.append((dr[..., None] == np.arange(n_dr)).astype(np.float32))
        ok.append(row_ok[:, None, :, None] & col_ok[None, :, None, :])
    row_sel = np.stack(row_sel)
    ok = np.stack(ok).reshape(3, 1, ATTN_Q, ATTN_KROWS * GRID_W)
    vals = jnp.einsum('pqka,hab,bcd->phqckd', row_sel, rel_bias.astype(F32), col_sel,
                      precision=lax.Precision.HIGHEST)
    vals = vals.reshape(3, N_HEADS, ATTN_Q, ATTN_KROWS * GRID_W)
    return jnp.where(ok, vals, NEG_INF)


def _attn_kernel(q_ref, k0_ref, k1_ref, k2_ref, v0_ref, v1_ref, v2_ref, km_ref, vm_ref, bias_ref, o_ref):
    nt = (((1,), (1,)), ((), ()))
    k_refs = (k0_ref, k1_ref, k2_ref)
    v_refs = (v0_ref, v1_ref, v2_ref)
    outs = []
    for h in range(N_HEADS):
        sl = slice(h * HEAD_DIM, (h + 1) * HEAD_DIM)
        qh = q_ref[:, sl]
        s_meta = lax.dot_general(qh, km_ref[:, sl], nt, preferred_element_type=F32)
        s_loc = [lax.dot_general(qh, k_refs[t][:, sl], nt, preferred_element_type=F32)
                 + bias_ref[0, h, :, t * ATTN_Q:(t + 1) * ATTN_Q] for t in range(ATTN_KBLK)]
        m_loc = functools.reduce(jnp.maximum, s_loc)
        m = jnp.maximum(jnp.max(s_meta, axis=-1, keepdims=True), jnp.max(m_loc, axis=-1, keepdims=True))
        p_meta = jnp.exp(s_meta - m)
        acc = jnp.dot(p_meta.astype(BF16), vm_ref[:, sl], preferred_element_type=F32)
        p_sum = None
        for t in range(ATTN_KBLK):
            p = jnp.exp(s_loc[t] - m)
            p_sum = p if p_sum is None else p_sum + p
            acc = acc + jnp.dot(p.astype(BF16), v_refs[t][:, sl], preferred_element_type=F32)
        denom = jnp.sum(p_meta, axis=-1, keepdims=True) + jnp.sum(p_sum, axis=-1, keepdims=True)
        outs.append(acc / denom)
    o_ref[...] = jnp.concatenate(outs, axis=-1).astype(BF16)


def _attention(q, k, v, km, vm, bias_pat, bsz, seq):
    nb = seq // ATTN_Q
    assert seq % ATTN_Q == 0 and nb >= ATTN_KBLK

    def q_map(b, j):
        return (b * nb + j, 0)

    def kv_map(t):
        return lambda b, j: (b * nb + jnp.clip(j - 1, 0, nb - ATTN_KBLK) + t, 0)

    def bias_map(b, j):
        return (jnp.where(j == 0, 0, jnp.where(j == nb - 1, 2, 1)), 0, 0, 0)

    blk = (ATTN_Q, ATTN_DIM)
    kv_specs = [pl.BlockSpec(blk, kv_map(t)) for t in range(ATTN_KBLK)]
    return pl.pallas_call(
        _attn_kernel,
        grid=(bsz, nb),
        in_specs=[pl.BlockSpec(blk, q_map)] + kv_specs + kv_specs
                 + [_const_spec((N_META, ATTN_DIM)), _const_spec((N_META, ATTN_DIM)),
                    pl.BlockSpec((1, N_HEADS, ATTN_Q, ATTN_KBLK * ATTN_Q), bias_map)],
        out_specs=pl.BlockSpec(blk, q_map),
        out_shape=jax.ShapeDtypeStruct((bsz * seq, ATTN_DIM), BF16),
        compiler_params=_params(("arbitrary", "arbitrary")),
        name="nbr_attn",
    )(q, k, k, k, v, v, v, km, vm, bias_pat)


def _meta_attn_kernel(q_ref, k_ref, v_ref, o_ref):
    nt = (((1,), (1,)), ((), ()))
    outs = []
    for h in range(N_HEADS):
        sl = slice(h * HEAD_DIM, (h + 1) * HEAD_DIM)
        s = lax.dot_general(q_ref[:, sl], k_ref[:, sl], nt, preferred_element_type=F32)
        p = jnp.exp(s - jnp.max(s, axis=-1, keepdims=True))
        o = jnp.dot(p.astype(BF16), v_ref[:, sl], preferred_element_type=F32)
        outs.append(o / jnp.sum(p, axis=-1, keepdims=True))
    o_ref[...] = jnp.concatenate(outs, axis=-1).astype(BF16)


def _meta_attention(qm, km, vm):
    spec = _const_spec((N_META, ATTN_DIM))
    return pl.pallas_call(
        _meta_attn_kernel,
        in_specs=[spec] * 3, out_specs=spec, grid=(1,),
        out_shape=jax.ShapeDtypeStruct((N_META, ATTN_DIM), BF16),
        compiler_params=_params(("arbitrary",)),
        name="meta_attn",
    )(qm, km, vm)


def _conv_kernel(c_ref, prev_ref, next_ref, cm_ref, w_ref, b_ref, lg_ref, lb_ref,
                 out_ref, outm_ref, xs_ref, mpad_ref):
    tile = c_ref.shape[0]
    n_pad = tile + 2 * CONV_HALO
    i = pl.program_id(1)
    first = i == 0
    last = i == pl.num_programs(1) - 1
    cm = cm_ref[...].astype(F32)
    xs_ref[0, 0:CONV_HALO, :] = jnp.where(first, cm, prev_ref[...].astype(F32))
    xs_ref[0, CONV_HALO:CONV_HALO + tile, :] = c_ref[...].astype(F32)
    xs_ref[0, CONV_HALO + tile:, :] = jnp.where(last, 0.0, next_ref[...].astype(F32))
    for s in range(1, SUBLANES):
        xs_ref[s, 0:n_pad - SUBLANES, :] = xs_ref[0, s:s + n_pad - SUBLANES, :]

    def conv_rows(read, rows):
        acc = jnp.zeros((rows, CONV_DIM), F32)
        for j in range(CONV_WIDTH):
            acc = acc + w_ref[j:j + 1, :] * read(j - CONV_HALF)
        y = acc + b_ref[...]
        mu = jnp.mean(y, axis=-1, keepdims=True)
        yc = y - mu
        var = jnp.mean(yc * yc, axis=-1, keepdims=True)
        z = yc * lax.rsqrt(var + EPS) * lg_ref[...] + lb_ref[...]
        return (z * jax.nn.sigmoid(z)).astype(BF16)

    for rc in range(tile // CONV_CHUNK):
        start = CONV_HALO + rc * CONV_CHUNK

        def read(d, start=start):
            lo = start + d
            return xs_ref[lo % SUBLANES, lo - lo % SUBLANES:lo - lo % SUBLANES + CONV_CHUNK, :]

        out_ref[rc * CONV_CHUNK:(rc + 1) * CONV_CHUNK, :] = conv_rows(read, CONV_CHUNK)

    @pl.when(first)
    def _():
        mpad_ref[0:N_META, :] = jnp.zeros((N_META, CONV_DIM), F32)
        mpad_ref[N_META:2 * N_META, :] = cm
        mpad_ref[2 * N_META:, :] = c_ref[0:N_META, :].astype(F32)
        outm_ref[...] = conv_rows(lambda d: mpad_ref[N_META + d:2 * N_META + d, :], N_META)


def _conv(c, cm, conv_w, conv_b, ln_g, ln_b, bsz, seq, tile):
    tps = seq // tile
    hpt = tile // CONV_HALO
    n_halo = bsz * seq // CONV_HALO

    def main_map(b, i):
        return (b * tps + i, 0)

    def prev_map(b, i):
        return (jnp.maximum((b * tps + i) * hpt - 1, 0), 0)

    def next_map(b, i):
        return (jnp.minimum((b * tps + i + 1) * hpt, n_halo - 1), 0)

    halo = (CONV_HALO, CONV_DIM)
    vec = _const_spec((1, CONV_DIM))
    return pl.pallas_call(
        _conv_kernel,
        grid=(bsz, tps),
        in_specs=[pl.BlockSpec((tile, CONV_DIM), main_map), pl.BlockSpec(halo, prev_map),
                  pl.BlockSpec(halo, next_map), _const_spec((N_META, CONV_DIM)),
                  _const_spec((CONV_WIDTH + 1, CONV_DIM)), vec, vec, vec],
        out_specs=[pl.BlockSpec((tile, CONV_DIM), main_map),
                   pl.BlockSpec((N_META, CONV_DIM), lambda b, i: (b, 0))],
        out_shape=[jax.ShapeDtypeStruct((bsz * seq, CONV_DIM), BF16),
                   jax.ShapeDtypeStruct((bsz * N_META, CONV_DIM), BF16)],
        scratch_shapes=[pltpu.VMEM((SUBLANES, tile + 2 * CONV_HALO, CONV_DIM), F32),
                        pltpu.VMEM((3 * N_META, CONV_DIM), F32)],
        compiler_params=_params(("arbitrary", "arbitrary")),
        name="conv_ln_silu",
    )(c, c, c, cm, conv_w, conv_b, ln_g, ln_b)


def _out_ffn_kernel(x_ref, a_ref, c_ref, wa_ref, wc_ref, g_ref, w1_ref, w3_ref, w2_ref, o_ref):
    x = (x_ref[...]
         + jnp.dot(a_ref[...], wa_ref[...], preferred_element_type=F32)
         + jnp.dot(c_ref[...], wc_ref[...], preferred_element_type=F32))
    n = _rms(x, g_ref[...]).astype(BF16)
    h1 = jnp.dot(n, w1_ref[...], preferred_element_type=F32)
    h3 = jnp.dot(n, w3_ref[...], preferred_element_type=F32)
    act = (h1 * jax.nn.sigmoid(h1) * h3).astype(BF16)
    o_ref[...] = x + jnp.dot(act, w2_ref[...], preferred_element_type=F32)


def _out_ffn(x, attn, cact, w_attn, w_conv, g, w1, w3, w2, tm):
    n = x.shape[0]
    wspec = _const_spec((ATTN_DIM, D_MODEL))
    return pl.pallas_call(
        _out_ffn_kernel,
        grid=(n // tm,),
        in_specs=[_row_spec(tm, D_MODEL), _row_spec(tm, ATTN_DIM), _row_spec(tm, CONV_DIM), wspec, wspec,
                  _const_spec((1, D_MODEL)), _const_spec((D_MODEL, D_FF)), _const_spec((D_MODEL, D_FF)),
                  _const_spec((D_FF, D_MODEL))],
        out_specs=_row_spec(tm, D_MODEL),
        out_shape=jax.ShapeDtypeStruct((n, D_MODEL), F32),
        compiler_params=_params(("arbitrary",)),
        name="out_proj_swiglu",
    )(x, attn, cact, w_attn, w_conv, g, w1, w3, w2)


def _pool_router_kernel(x_ref, prev_ref, next_ref, xm_ref, g_ref, wp_ref, ps_ref, g2_ref, rhi_ref, rlo_ref,
                        x3_ref, hn_ref, gate_ref, pos_ref, pos_t_ref, cnt_ref, npad_ref, *, seq):
    tile = x_ref.shape[0]
    i = pl.program_id(1)
    first = i == 0
    last = i == pl.num_programs(1) - 1
    gain = g_ref[...]
    x = x_ref[...]
    n_main = _rms(x, gain)
    npad_ref[0:POOL_HALO, :] = _rms(jnp.where(first, xm_ref[...], prev_ref[...]), gain)
    npad_ref[POOL_HALO:POOL_HALO + tile, :] = n_main
    npad_ref[POOL_HALO + tile:, :] = jnp.where(last, 0.0, _rms(next_ref[...], gain))

    tok = i * tile + lax.broadcasted_iota(jnp.int32, (tile, 1), 0)
    mixed = []
    for gi, w in enumerate(POOL_WINDOWS):
        cols = slice(gi * POOL_GROUP_DIM, (gi + 1) * POOL_GROUP_DIM)
        half = w // 2
        acc = jnp.zeros((tile, POOL_GROUP_DIM), F32)
        for d in range(-half, half):
            acc = acc + npad_ref[POOL_HALO + d:POOL_HALO + d + tile, cols]
        count = (w - jnp.maximum(tok + half - seq, 0)).astype(F32)
        diff = (acc / count - n_main[:, cols]).astype(BF16)
        mixed.append(jnp.dot(diff, wp_ref[gi], preferred_element_type=F32))
    x3 = x + jnp.concatenate(mixed, axis=-1) * ps_ref[...]
    x3_ref[...] = x3

    hn = _rms(x3, g2_ref[...])
    hn_ref[...] = hn
    h_hi = hn.astype(BF16)
    h_lo = (hn - h_hi.astype(F32)).astype(BF16)
    logits = (jnp.dot(h_hi, rhi_ref[...], preferred_element_type=F32)
              + jnp.dot(h_lo, rhi_ref[...], preferred_element_type=F32)
              + jnp.dot(h_hi, rlo_ref[...], preferred_element_type=F32))
    lane = lax.broadcasted_iota(jnp.int32, logits.shape, 1)
    lane_f = lane.astype(F32)
    logits = jnp.where(lane < N_EXPERTS, logits, NEG_INF)
    m1 = jnp.max(logits, axis=-1, keepdims=True)
    i1 = jnp.min(jnp.where(logits == m1, lane_f, 256.0), axis=-1, keepdims=True)
    rest = jnp.where(lane_f == i1, NEG_INF, logits)
    m2 = jnp.max(rest, axis=-1, keepdims=True)
    i2 = jnp.min(jnp.where(rest == m2, lane_f, 256.0), axis=-1, keepdims=True)
    e2 = jnp.exp(m2 - m1)
    g1 = 1.0 / (1.0 + e2)
    gate_ref[...] = jnp.where(lane == 0, g1, jnp.where(lane == 1, e2 * g1, 0.0))

    hot = [jnp.where(lane_f == i1, 1.0, 0.0), jnp.where(lane_f == i2, 1.0, 0.0)]
    earlier = jnp.where(lax.broadcasted_iota(jnp.int32, (tile, tile), 1)
                        < lax.broadcasted_iota(jnp.int32, (tile, tile), 0), 1.0, 0.0).astype(BF16)
    before = [jnp.dot(earlier, h.astype(BF16), preferred_element_type=F32) for h in hot]
    count = [jnp.sum(h, axis=0, keepdims=True) for h in hot]
    total = count[0] + count[1]
    chunks = jnp.floor((total + (MOE_CHUNK - 1)) * (1.0 / MOE_CHUNK))
    lower_expert = jnp.where(lax.broadcasted_iota(jnp.int32, (128, 128), 0)
                             < lax.broadcasted_iota(jnp.int32, (128, 128), 1), 1.0, 0.0).astype(BF16)
    seg_off = MOE_CHUNK * jnp.dot(jnp.broadcast_to(chunks, (SUBLANES, 128)).astype(BF16), lower_expert,
                                  preferred_element_type=F32)[0:1, :]
    pos0 = jnp.sum(hot[0] * (before[0] + seg_off), axis=-1, keepdims=True)
    pos1 = jnp.sum(hot[1] * (before[1] + count[0] + seg_off), axis=-1, keepdims=True)
    pos = jnp.where(lane == 0, pos0, jnp.where(lane == 1, pos1, 0.0))
    pos_ref[...] = pos.astype(jnp.int32)
    pos_t_ref[...] = jnp.transpose(pos)[0:SUBLANES, :].astype(jnp.int32)
    cnt_ref[...] = jnp.broadcast_to(total, (SUBLANES, 128)).astype(jnp.int32)


def _pool_router(x2, x2m, g, w_pool, pool_scale, g2, r_hi, r_lo, bsz, seq, tile):
    tps = seq // tile
    hpt = tile // POOL_HALO
    n_halo = bsz * seq // POOL_HALO

    def main_map(b, i):
        return (b * tps + i, 0)

    def prev_map(b, i):
        return (jnp.maximum((b * tps + i) * hpt - 1, 0), 0)

    def next_map(b, i):
        return (jnp.minimum((b * tps + i + 1) * hpt, n_halo - 1), 0)

    halo = (POOL_HALO, D_MODEL)
    vec = _const_spec((1, D_MODEL))
    main = pl.BlockSpec((tile, D_MODEL), main_map)
    small = pl.BlockSpec((tile, 128), main_map)
    n = bsz * seq
    return pl.pallas_call(
        functools.partial(_pool_router_kernel, seq=seq),
        grid=(bsz, tps),
        in_specs=[main, pl.BlockSpec(halo, prev_map), pl.BlockSpec(halo, next_map),
                  pl.BlockSpec(halo, lambda b, i: (2 * b + 1, 0)), vec,
                  _const_spec((len(POOL_WINDOWS), POOL_GROUP_DIM, POOL_GROUP_DIM)), vec, vec,
                  _const_spec((D_MODEL, 128)), _const_spec((D_MODEL, 128))],
        out_specs=[main, main, small, small,
                   pl.BlockSpec((SUBLANES, tile), lambda b, i: (0, b * tps + i)),
                   pl.BlockSpec((None, SUBLANES, 128), lambda b, i: (b * tps + i, 0, 0))],
        out_shape=[jax.ShapeDtypeStruct((n, D_MODEL), F32), jax.ShapeDtypeStruct((n, D_MODEL), F32),
                   jax.ShapeDtypeStruct((n, 128), F32), jax.ShapeDtypeStruct((n, 128), jnp.int32),
                   jax.ShapeDtypeStruct((SUBLANES, n), jnp.int32),
                   jax.ShapeDtypeStruct((n // tile, SUBLANES, 128), jnp.int32)],
        scratch_shapes=[pltpu.VMEM((tile + 2 * POOL_HALO, D_MODEL), F32)],
        compiler_params=_params(("arbitrary", "arbitrary")),
        name="pool_router",
    )(x2, x2, x2, x2m, g, w_pool, pool_scale, g2, r_hi, r_lo)


def _split_chunks(n_chunks):
    per_big = MOE_BIG // MOE_CHUNK
    n_big = lax.div(n_chunks, per_big)
    return n_big, n_chunks - n_big * per_big


def _start_pieces(copy, first_a, first_b, n_chunks):
    n_big, n_small = _split_chunks(n_chunks)

    def start(rows, base_a, base_b):
        def body(c, carry):
            copy(pl.multiple_of(base_a + c * rows, MOE_CHUNK), pl.multiple_of(base_b + c * rows, MOE_CHUNK),
                 rows).start()
            return carry
        return body

    lax.fori_loop(0, n_big, start(MOE_BIG, first_a, first_b), 0)
    lax.fori_loop(0, n_small, start(MOE_CHUNK, first_a + n_big * MOE_BIG, first_b + n_big * MOE_BIG), 0)


def _wait_pieces(copy, chunk_counts):
    splits = [_split_chunks(n) for n in chunk_counts]
    for rows, count in ((MOE_BIG, sum(s[0] for s in splits)), (MOE_CHUNK, sum(s[1] for s in splits))):
        def wait(c, carry, rows=rows):
            copy(0, 0, rows).wait()
            return carry

        lax.fori_loop(0, count, wait, 0)


def _start_segments(copy, off_ref, slot_ref, nch_ref, t):
    for e in range(N_EXPERTS):
        _start_pieces(copy, off_ref[t * N_EXPERTS + e], slot_ref[t * N_EXPERTS + e], nch_ref[t * N_EXPERTS + e])


def _wait_segments(copy, nch_ref, t):
    _wait_pieces(copy, [nch_ref[t * N_EXPERTS + e] for e in range(N_EXPERTS)])


def _dispatch_kernel(off_ref, slot_ref, nch_ref, tail_ref, ntail_ref, pos_ref, *refs, group_tiles):
    n_groups = len(group_tiles) - 1
    hn_refs = refs[:n_groups]
    xs_ref, buf_ref, zero_ref, sem = refs[n_groups:]
    t = pl.program_id(0)
    cur = lax.rem(t, 2)

    def copy_from(b):
        def copy(buf_row, slot_row, rows):
            return pltpu.make_async_copy(buf_ref.at[b, pl.ds(buf_row, rows)], xs_ref.at[pl.ds(slot_row, rows)],
                                         sem.at[b])
        return copy

    @pl.when(t >= 2)
    def _():
        _wait_segments(copy_from(cur), nch_ref, t - 2)

    rows = lax.broadcasted_iota(jnp.int32, (MOE_ROWS, MOE_TILE), 0)
    perm = jnp.where(rows == pos_ref[0:1, :], 1.0, jnp.where(rows == pos_ref[1:2, :], 1.0, 0.0)).astype(BF16)
    for g in range(n_groups):
        @pl.when((t >= group_tiles[g]) & (t < group_tiles[g + 1]))
        def _(g=g):
            buf_ref[cur] = jnp.dot(perm, hn_refs[g][...].astype(BF16), preferred_element_type=F32)

    _start_segments(copy_from(cur), off_ref, slot_ref, nch_ref, t)

    @pl.when(t == pl.num_programs(0) - 1)
    def _():
        @pl.when(t >= 1)
        def _():
            _wait_segments(copy_from(1 - cur), nch_ref, t - 1)

        _wait_segments(copy_from(cur), nch_ref, t)

        zero_ref[...] = jnp.zeros(zero_ref.shape, F32)

        def fill(unused_row, slot_row, rows):
            del unused_row
            return pltpu.make_async_copy(zero_ref.at[pl.ds(0, rows)], xs_ref.at[pl.ds(slot_row, rows)], sem.at[0])

        for e in range(N_EXPERTS):
            _start_pieces(fill, 0, tail_ref[e], ntail_ref[e])
        _wait_pieces(fill, [ntail_ref[e] for e in range(N_EXPERTS)])


def _dispatch(seg_tables, tail_tables, pos_t, hns, n_slots):
    group_tiles = [0]
    for hn in hns:
        group_tiles.append(group_tiles[-1] + hn.shape[0] // MOE_TILE)

    def hn_spec(g):
        first, count = group_tiles[g], group_tiles[g + 1] - group_tiles[g]
        return pl.BlockSpec((MOE_TILE, D_MODEL), lambda i, *_: (jnp.clip(i - first, 0, count - 1), 0))

    grid_spec = pltpu.PrefetchScalarGridSpec(
        num_scalar_prefetch=5,
        grid=(group_tiles[-1],),
        in_specs=[pl.BlockSpec((SUBLANES, MOE_TILE), lambda i, *_: (0, i))] + [hn_spec(g) for g in range(len(hns))],
        out_specs=pl.BlockSpec(memory_space=pl.ANY),
        scratch_shapes=[pltpu.VMEM((2, MOE_ROWS, D_MODEL), F32), pltpu.VMEM((MOE_BIG, D_MODEL), F32),
                        pltpu.SemaphoreType.DMA((2,))],
    )
    return pl.pallas_call(
        functools.partial(_dispatch_kernel, group_tiles=tuple(group_tiles)),
        grid_spec=grid_spec,
        out_shape=jax.ShapeDtypeStruct((n_slots, D_MODEL), F32),
        compiler_params=_params(("arbitrary",)),
        name="moe_dispatch",
    )(*seg_tables, *tail_tables, pos_t, *hns)


def _expert_kernel(te_ref, tv_ref, x_ref, w1_ref, w3_ref, w2_ref, y_ref, act_ref):
    del te_ref
    valid = tv_ref[pl.program_id(0)] == 1
    n_chunks = act_ref.shape[0]

    @pl.when(valid)
    def _():
        xb = x_ref[...].astype(BF16)
        for c in range(n_chunks):
            cols = slice(c * EXPERT_FF_CHUNK, (c + 1) * EXPERT_FF_CHUNK)
            h1 = jnp.dot(xb, w1_ref[:, cols], preferred_element_type=F32)
            h3 = jnp.dot(xb, w3_ref[:, cols], preferred_element_type=F32)
            act_ref[c] = (h1 * jax.nn.sigmoid(h1) * h3).astype(BF16)
        acc = jnp.dot(act_ref[0], w2_ref[0:EXPERT_FF_CHUNK, :], preferred_element_type=F32)
        for c in range(1, n_chunks):
            acc = acc + jnp.dot(act_ref[c], w2_ref[c * EXPERT_FF_CHUNK:(c + 1) * EXPERT_FF_CHUNK, :],
                                preferred_element_type=F32)
        y_ref[...] = acc

    @pl.when(jnp.logical_not(valid))
    def _():
        y_ref[...] = jnp.zeros(y_ref.shape, F32)


def _experts(tile_expert, tile_valid, xs, w1, w3, w2):
    n_slots = xs.shape[0]
    per_expert = dict(pipeline_mode=pl.Buffered(1))
    grid_spec = pltpu.PrefetchScalarGridSpec(
        num_scalar_prefetch=2,
        grid=(n_slots // EXPERT_TILE,),
        in_specs=[pl.BlockSpec((EXPERT_TILE, D_MODEL), lambda j, te, tv: (jnp.where(tv[j] == 1, j, 0), 0)),
                  pl.BlockSpec((None, D_MODEL, D_FF_EXPERT), lambda j, te, tv: (te[j], 0, 0), **per_expert),
                  pl.BlockSpec((None, D_MODEL, D_FF_EXPERT), lambda j, te, tv: (te[j], 0, 0), **per_expert),
                  pl.BlockSpec((None, D_FF_EXPERT, D_MODEL), lambda j, te, tv: (te[j], 0, 0), **per_expert)],
        out_specs=pl.BlockSpec((EXPERT_TILE, D_MODEL), lambda j, te, tv: (j, 0)),
        scratch_shapes=[pltpu.VMEM((D_FF_EXPERT // EXPERT_FF_CHUNK, EXPERT_TILE, EXPERT_FF_CHUNK), BF16)],
    )
    return pl.pallas_call(
        _expert_kernel,
        grid_spec=grid_spec,
        out_shape=jax.ShapeDtypeStruct((n_slots, D_MODEL), F32),
        compiler_params=_params(("arbitrary",)),
        name="moe_experts",
    )(tile_expert, tile_valid, xs, w1, w3, w2)


def _combine_kernel(off_ref, slot_ref, nch_ref, pos_ref, x_ref, gate_ref, y_ref, o_ref, buf_ref, sem):
    t = pl.program_id(0)
    cur = lax.rem(t, 2)

    def copy_into(b):
        def copy(buf_row, slot_row, rows):
            return pltpu.make_async_copy(y_ref.at[pl.ds(slot_row, rows)], buf_ref.at[b, pl.ds(buf_row, rows)],
                                         sem.at[b])
        return copy

    @pl.when(t == 0)
    def _():
        buf_ref[...] = jnp.zeros(buf_ref.shape, F32)
        _start_segments(copy_into(0), off_ref, slot_ref, nch_ref, 0)

    @pl.when(t + 1 < pl.num_programs(0))
    def _():
        _start_segments(copy_into(1 - cur), off_ref, slot_ref, nch_ref, t + 1)

    _wait_segments(copy_into(cur), nch_ref, t)
    yb = buf_ref[cur].astype(BF16)
    cols = lax.broadcasted_iota(jnp.int32, (MOE_TILE, MOE_ROWS), 1)
    pos = pos_ref[...]
    gates = gate_ref[...]
    out = x_ref[...]
    for k in range(TOP_K):
        pick = jnp.where(cols == pos[:, k:k + 1], 1.0, 0.0).astype(BF16)
        out = out + gates[:, k:k + 1] * jnp.dot(pick, yb, preferred_element_type=F32)
    o_ref[...] = out


def _combine(seg_tables, pos, x3, gates, y):
    n = x3.shape[0]
    grid_spec = pltpu.PrefetchScalarGridSpec(
        num_scalar_prefetch=3,
        grid=(n // MOE_TILE,),
        in_specs=[pl.BlockSpec((MOE_TILE, 128), lambda i, *_: (i, 0)),
                  pl.BlockSpec((MOE_TILE, D_MODEL), lambda i, *_: (i, 0)),
                  pl.BlockSpec((MOE_TILE, 128), lambda i, *_: (i, 0)),
                  pl.BlockSpec(memory_space=pl.ANY)],
        out_specs=pl.BlockSpec((MOE_TILE, D_MODEL), lambda i, *_: (i, 0)),
        scratch_shapes=[pltpu.VMEM((2, MOE_ROWS, D_MODEL), F32), pltpu.SemaphoreType.DMA((2,))],
    )
    return pl.pallas_call(
        _combine_kernel,
        grid_spec=grid_spec,
        out_shape=jax.ShapeDtypeStruct((n, D_MODEL), F32),
        compiler_params=_params(("arbitrary",)),
        name="moe_combine",
    )(*seg_tables, pos, x3, gates, y)


def _route(counts):
    n_tok_tiles = counts.shape[0]
    seg = (counts + MOE_CHUNK - 1) // MOE_CHUNK * MOE_CHUNK
    seg_off = jnp.cumsum(seg, axis=1) - seg
    used = jnp.sum(seg, axis=0)
    padded = (used + EXPERT_TILE - 1) // EXPERT_TILE * EXPERT_TILE
    pad_end = jnp.cumsum(padded)
    pad_start = pad_end - padded
    seg_slot = pad_start[None, :] + jnp.cumsum(seg, axis=0) - seg
    worst_rows = n_tok_tiles * (TOP_K * MOE_TILE + N_EXPERTS * (MOE_CHUNK - 1)) + N_EXPERTS * (EXPERT_TILE - 1)
    n_tiles = (worst_rows + EXPERT_TILE - 1) // EXPERT_TILE
    tile_start = jnp.arange(n_tiles, dtype=jnp.int32) * EXPERT_TILE
    tile_expert = jnp.sum((tile_start[:, None] >= pad_end[None, :]).astype(jnp.int32), axis=1)
    tile_expert = jnp.minimum(tile_expert, N_EXPERTS - 1)
    tile_valid = (tile_start < pad_end[-1]).astype(jnp.int32)
    segs = (seg_off, seg_slot, seg // MOE_CHUNK)
    tail_end = pad_end.at[N_EXPERTS - 1].set(n_tiles * EXPERT_TILE)
    tails = (pad_start + used, (tail_end - pad_start - used) // MOE_CHUNK)
    return segs, tails, tile_expert, tile_valid, n_tiles


def _trunk(groups, meta_tokens, norm_mix_e, w_in, q_gain, k_gain, rel_bias, conv_w, conv_b,
           conv_ln_g, conv_ln_b, w_out, norm_ffn_e, ffn_w1, ffn_w3, ffn_w2,
           norm_mix_o, pool_w, pool_scale, norm_ffn_o, router_w, moe_w1, moe_w3, moe_w2):
    row = lambda a: a.reshape(1, -1).astype(F32)
    shapes = [(g.shape[0], g.shape[1]) for g in groups]
    xs = [g.reshape(-1, D_MODEL) for g in groups]

    hsum = jnp.asarray(np.kron(np.eye(N_HEADS), np.ones((HEAD_DIM, HEAD_DIM))), BF16)
    qg = row(jnp.tile(q_gain[0], N_HEADS))
    kg = row(jnp.tile(k_gain[0], N_HEADS))
    w_in_b = w_in[0].astype(BF16)
    g_mix = row(norm_mix_e[0])
    bias_pat = _attn_bias_patterns(rel_bias[0])
    conv_w_p = jnp.concatenate([conv_w[0].astype(F32), jnp.zeros((1, CONV_DIM), F32)], axis=0)
    w_out_a = w_out[0, :ATTN_DIM].astype(BF16)
    w_out_c = w_out[0, ATTN_DIM:].astype(BF16)
    g_ffn = row(norm_ffn_e[0])
    w1b, w3b, w2b = ffn_w1[0].astype(BF16), ffn_w3[0].astype(BF16), ffn_w2[0].astype(BF16)

    meta = meta_tokens.astype(F32)
    qm, km, vm, cm = _proj(meta, g_mix, w_in_b, qg, kg, hsum, N_META)
    attn_m = _meta_attention(qm, km, vm)

    x2s, x2ms = [], []
    for x, (bsz, seq) in zip(xs, shapes):
        q, k, v, c = _proj(x, g_mix, w_in_b, qg, kg, hsum, TOKEN_TILE)
        attn = _attention(q, k, v, km, vm, bias_pat, bsz, seq)
        cact, cact_m = _conv(c, cm, conv_w_p, row(conv_b[0]), row(conv_ln_g[0]), row(conv_ln_b[0]),
                             bsz, seq, TOKEN_TILE)
        x2s.append(_out_ffn(x, attn, cact, w_out_a, w_out_c, g_ffn, w1b, w3b, w2b, TOKEN_TILE))
        x2ms.append(_out_ffn(jnp.tile(meta, (bsz, 1)), jnp.tile(attn_m, (bsz, 1)), cact_m, w_out_a, w_out_c,
                             g_ffn, w1b, w3b, w2b, bsz * N_META))

    g_pool = row(norm_mix_o[0])
    w_pool_b = pool_w[0].astype(BF16)
    g_moe = row(norm_ffn_o[0])
    rw = jnp.zeros((D_MODEL, 128), F32).at[:, :N_EXPERTS].set(router_w[0].astype(F32))
    r_hi = rw.astype(BF16)
    r_lo = (rw - r_hi.astype(F32)).astype(BF16)
    mw1, mw3, mw2 = moe_w1[0].astype(BF16), moe_w3[0].astype(BF16), moe_w2[0].astype(BF16)

    assert TOKEN_TILE == MOE_TILE
    routed = [_pool_router(x2, x2m, g_pool, w_pool_b, row(pool_scale[0]), g_moe, r_hi, r_lo, bsz, seq, MOE_TILE)
              for x2, x2m, (bsz, seq) in zip(x2s, x2ms, shapes)]
    counts = jnp.concatenate([r[5][:, 0, :N_EXPERTS] for r in routed], axis=0)
    segs, tails, tile_expert, tile_valid, n_tiles = _route(counts)
    slots = _dispatch(tuple(a.reshape(-1) for a in segs), tails,
                      jnp.concatenate([r[4] for r in routed], axis=1), [r[1] for r in routed],
                      n_tiles * EXPERT_TILE)
    y = _experts(tile_expert, tile_valid, slots, mw1, mw3, mw2)
    outs, t0 = [], 0
    for (x3, _, gate, pos, _, cnt), (bsz, seq) in zip(routed, shapes):
        t1 = t0 + cnt.shape[0]
        seg_tables = tuple(a[t0:t1].reshape(-1) for a in segs)
        outs.append(_combine(seg_tables, pos, x3, gate, y).reshape(bsz, seq, D_MODEL))
        t0 = t1
    return tuple(outs)


def kernel(x_prompt, x_sample, meta_tokens, norm_mix_e, w_in, q_gain, k_gain, rel_bias, conv_w, conv_b, conv_ln_g, conv_ln_b, w_out, norm_ffn_e, ffn_w1, ffn_w3, ffn_w2, norm_mix_o, pool_w, pool_scale, norm_ffn_o, router_w, moe_w1, moe_w3, moe_w2):
    return _trunk([x_prompt, x_sample], meta_tokens, norm_mix_e, w_in, q_gain, k_gain, rel_bias, conv_w,
                  conv_b, conv_ln_g, conv_ln_b, w_out, norm_ffn_e, ffn_w1, ffn_w3, ffn_w2,
                  norm_mix_o, pool_w, pool_scale, norm_ffn_o, router_w, moe_w1, moe_w3, moe_w2)
```

```python
import functools

import numpy as np
import jax
import jax.numpy as jnp
from jax import lax
from jax.experimental import pallas as pl
from jax.experimental.pallas import tpu as pltpu

D_MODEL = 1024
N_META = 16
GRID_W = 64
N_HEADS = 8
HEAD_DIM = 64
ATTN_DIM = N_HEADS * HEAD_DIM
CONV_DIM = D_MODEL - ATTN_DIM
IN_DIM = 3 * ATTN_DIM + 2 * CONV_DIM
WIN_ROWS = 8
WIN_COLS = 16
CONV_WIDTH = 31
CONV_HALF = CONV_WIDTH // 2
POOL_WINDOWS = (2, 4, 8, 16)
POOL_GROUP_DIM = D_MODEL // len(POOL_WINDOWS)
D_FF = 2816
N_EXPERTS = 8
TOP_K = 2
D_FF_EXPERT = 3584
EPS = 1e-6

F32 = jnp.float32
BF16 = jnp.bfloat16
SUBLANES = 8

TOKEN_TILE = 512
ATTN_ROWS = 4
ATTN_Q = ATTN_ROWS * GRID_W
ATTN_KROWS = ATTN_ROWS + WIN_ROWS
ATTN_KBLK = ATTN_KROWS * GRID_W // ATTN_Q
CONV_HALO = 16
CONV_CHUNK = 64
POOL_HALO = 8
EXPERT_TILE = 1024
EXPERT_FF_CHUNK = 512
MOE_TILE = 512
MOE_CHUNK = 8
MOE_BIG = 32
MOE_ROWS = 1152
NEG_INF = -1e30
VMEM_LIMIT = 56 * 1024 * 1024


def _params(sem):
    return pltpu.CompilerParams(dimension_semantics=sem, vmem_limit_bytes=VMEM_LIMIT)


def _const_spec(shape):
    zeros = (0,) * len(shape)
    return pl.BlockSpec(shape, lambda *_: zeros)


def _row_spec(tm, cols):
    return pl.BlockSpec((tm, cols), lambda i: (i, 0))


def _rms(x, gain):
    ms = jnp.mean(x * x, axis=-1, keepdims=True)
    return x * lax.rsqrt(ms + EPS) * gain


def _proj_kernel(x_ref, g_ref, w_ref, qg_ref, kg_ref, hsum_ref, q_ref, k_ref, v_ref, c_ref):
    n = _rms(x_ref[...], g_ref[...]).astype(BF16)
    proj = jnp.dot(n, w_ref[...], preferred_element_type=F32)
    q = proj[:, :ATTN_DIM]
    k = proj[:, ATTN_DIM:2 * ATTN_DIM]
    v = proj[:, 2 * ATTN_DIM:3 * ATTN_DIM]
    u = proj[:, 3 * ATTN_DIM:3 * ATTN_DIM + CONV_DIM]
    g = proj[:, 3 * ATTN_DIM + CONV_DIM:]

    def head_norm(a, gain):
        ss = jnp.dot((a * a).astype(BF16), hsum_ref[...], preferred_element_type=F32)
        return a * lax.rsqrt(ss * (1.0 / HEAD_DIM) + EPS) * gain

    q_ref[...] = (head_norm(q, qg_ref[...]) * (HEAD_DIM ** -0.5)).astype(BF16)
    k_ref[...] = head_norm(k, kg_ref[...]).astype(BF16)
    v_ref[...] = v.astype(BF16)
    c_ref[...] = (u * jax.nn.sigmoid(g)).astype(BF16)


def _proj(x, norm_g, w_in, qg, kg, hsum, tm):
    n = x.shape[0]
    out = jax.ShapeDtypeStruct((n, ATTN_DIM), BF16)
    return pl.pallas_call(
        _proj_kernel,
        grid=(n // tm,),
        in_specs=[_row_spec(tm, D_MODEL), _const_spec((1, D_MODEL)), _const_spec((D_MODEL, IN_DIM)),
                  _const_spec((1, ATTN_DIM)), _const_spec((1, ATTN_DIM)), _const_spec((ATTN_DIM, ATTN_DIM))],
        out_specs=[_row_spec(tm, ATTN_DIM)] * 4,
        out_shape=[out] * 4,
        compiler_params=_params(("arbitrary",)),
        name="in_proj",
    )(x, norm_g, w_in, qg, kg, hsum)


def _attn_bias_patterns(rel_bias):
    n_dr, n_dc = 2 * WIN_ROWS - 1, 2 * WIN_COLS - 1
    qr = np.arange(ATTN_ROWS)[:, None]
    kr = np.arange(ATTN_KROWS)[None, :]
    qc = np.arange(GRID_W)[:, None]
    kc = np.arange(GRID_W)[None, :]
    cs = np.clip(qc - WIN_COLS // 2, 0, GRID_W - WIN_COLS)
    col_ok = (kc >= cs) & (kc < cs + WIN_COLS)
    dc = np.clip(kc - qc + (WIN_COLS - 1), 0, n_dc - 1)
    col_sel = (dc[None] == np.arange(n_dc)[:, None, None]).astype(np.float32)
    row_sel, ok = [], []
    for off, rs in ((0, 0 * qr), (ATTN_ROWS, qr), (2 * ATTN_ROWS, ATTN_ROWS + 0 * qr)):
        row_ok = (kr >= rs) & (kr < rs + WIN_ROWS)
        dr = np.clip(kr - (off + qr) + (WIN_ROWS - 1), 0, n_dr - 1)
        row_sel.append((dr[..., None] == np.arange(n_dr)).astype(np.float32))
        ok.append(row_ok[:, None, :, None] & col_ok[None, :, None, :])
    row_sel = np.stack(row_sel)
    ok = np.stack(ok).reshape(3, 1, ATTN_Q, ATTN_KROWS * GRID_W)
    vals = jnp.einsum('pqka,hab,bcd->phqckd', row_sel, rel_bias.astype(F32), col_sel,
                      precision=lax.Precision.HIGHEST)
    vals = vals.reshape(3, N_HEADS, ATTN_Q, ATTN_KROWS * GRID_W)
    return jnp.where(ok, vals, NEG_INF)


def _attn_kernel(q_ref, k0_ref, k1_ref, k2_ref, v0_ref, v1_ref, v2_ref, km_ref, vm_ref, bias_ref, o_ref):
    nt = (((1,), (1,)), ((), ()))
    k_refs = (k0_ref, k1_ref, k2_ref)
    v_refs = (v0_ref, v1_ref, v2_ref)
    outs = []
    for h in range(N_HEADS):
        sl = slice(h * HEAD_DIM, (h + 1) * HEAD_DIM)
        qh = q_ref[:, sl]
        s_meta = lax.dot_general(qh, km_ref[:, sl], nt, preferred_element_type=F32)
        s_loc = [lax.dot_general(qh, k_refs[t][:, sl], nt, preferred_element_type=F32)
                 + bias_ref[0, h, :, t * ATTN_Q:(t + 1) * ATTN_Q] for t in range(ATTN_KBLK)]
        m_loc = functools.reduce(jnp.maximum, s_loc)
        m = jnp.maximum(jnp.max(s_meta, axis=-1, keepdims=True), jnp.max(m_loc, axis=-1, keepdims=True))
        p_meta = jnp.exp(s_meta - m)
        acc = jnp.dot(p_meta.astype(BF16), vm_ref[:, sl], preferred_element_type=F32)
        p_sum = None
        for t in range(ATTN_KBLK):
            p = jnp.exp(s_loc[t] - m)
            p_sum = p if p_sum is None else p_sum + p
            acc = acc + jnp.dot(p.astype(BF16), v_refs[t][:, sl], preferred_element_type=F32)
        denom = jnp.sum(p_meta, axis=-1, keepdims=True) + jnp.sum(p_sum, axis=-1, keepdims=True)
        outs.append(acc / denom)
    o_ref[...] = jnp.concatenate(outs, axis=-1).astype(BF16)


def _attention(q, k, v, km, vm, bias_pat, bsz, seq):
    nb = seq // ATTN_Q
    assert seq % ATTN_Q == 0 and nb >= ATTN_KBLK

    def q_map(b, j):
        return (b * nb + j, 0)

    def kv_map(t):
        return lambda b, j: (b * nb + jnp.clip(j - 1, 0, nb - ATTN_KBLK) + t, 0)

    def bias_map(b, j):
        return (jnp.where(j == 0, 0, jnp.where(j == nb - 1, 2, 1)), 0, 0, 0)

    blk = (ATTN_Q, ATTN_DIM)
    kv_specs = [pl.BlockSpec(blk, kv_map(t)) for t in range(ATTN_KBLK)]
    return pl.pallas_call(
        _attn_kernel,
        grid=(bsz, nb),
        in_specs=[pl.BlockSpec(blk, q_map)] + kv_specs + kv_specs
                 + [_const_spec((N_META, ATTN_DIM)), _const_spec((N_META, ATTN_DIM)),
                    pl.BlockSpec((1, N_HEADS, ATTN_Q, ATTN_KBLK * ATTN_Q), bias_map)],
        out_specs=pl.BlockSpec(blk, q_map),
        out_shape=jax.ShapeDtypeStruct((bsz * seq, ATTN_DIM), BF16),
        compiler_params=_params(("arbitrary", "arbitrary")),
        name="nbr_attn",
    )(q, k, k, k, v, v, v, km, vm, bias_pat)


def _meta_attn_kernel(q_ref, k_ref, v_ref, o_ref):
    nt = (((1,), (1,)), ((), ()))
    outs = []
    for h in range(N_HEADS):
        sl = slice(h * HEAD_DIM, (h + 1) * HEAD_DIM)
        s = lax.dot_general(q_ref[:, sl], k_ref[:, sl], nt, preferred_element_type=F32)
        p = jnp.exp(s - jnp.max(s, axis=-1, keepdims=True))
        o = jnp.dot(p.astype(BF16), v_ref[:, sl], preferred_element_type=F32)
        outs.append(o / jnp.sum(p, axis=-1, keepdims=True))
    o_ref[...] = jnp.concatenate(outs, axis=-1).astype(BF16)


def _meta_attention(qm, km, vm):
    spec = _const_spec((N_META, ATTN_DIM))
    return pl.pallas_call(
        _meta_attn_kernel,
        in_specs=[spec] * 3, out_specs=spec, grid=(1,),
        out_shape=jax.ShapeDtypeStruct((N_META, ATTN_DIM), BF16),
        compiler_params=_params(("arbitrary",)),
        name="meta_attn",
    )(qm, km, vm)


def _conv_kernel(c_ref, prev_ref, next_ref, cm_ref, w_ref, b_ref, lg_ref, lb_ref,
                 out_ref, outm_ref, xs_ref, mpad_ref):
    tile = c_ref.shape[0]
    n_pad = tile + 2 * CONV_HALO
    i = pl.program_id(1)
    first = i == 0
    last = i == pl.num_programs(1) - 1
    cm = cm_ref[...].astype(F32)
    xs_ref[0, 0:CONV_HALO, :] = jnp.where(first, cm, prev_ref[...].astype(F32))
    xs_ref[0, CONV_HALO:CONV_HALO + tile, :] = c_ref[...].astype(F32)
    xs_ref[0, CONV_HALO + tile:, :] = jnp.where(last, 0.0, next_ref[...].astype(F32))
    for s in range(1, SUBLANES):
        xs_ref[s, 0:n_pad - SUBLANES, :] = xs_ref[0, s:s + n_pad - SUBLANES, :]

    def conv_rows(read, rows):
        acc = jnp.zeros((rows, CONV_DIM), F32)
        for j in range(CONV_WIDTH):
            acc = acc + w_ref[j:j + 1, :] * read(j - CONV_HALF)
        y = acc + b_ref[...]
        mu = jnp.mean(y, axis=-1, keepdims=True)
        yc = y - mu
        var = jnp.mean(yc * yc, axis=-1, keepdims=True)
        z = yc * lax.rsqrt(var + EPS) * lg_ref[...] + lb_ref[...]
        return (z * jax.nn.sigmoid(z)).astype(BF16)

    for rc in range(tile // CONV_CHUNK):
        start = CONV_HALO + rc * CONV_CHUNK

        def read(d, start=start):
            lo = start + d
            return xs_ref[lo % SUBLANES, lo - lo % SUBLANES:lo - lo % SUBLANES + CONV_CHUNK, :]

        out_ref[rc * CONV_CHUNK:(rc + 1) * CONV_CHUNK, :] = conv_rows(read, CONV_CHUNK)

    @pl.when(first)
    def _():
        mpad_ref[0:N_META, :] = jnp.zeros((N_META, CONV_DIM), F32)
        mpad_ref[N_META:2 * N_META, :] = cm
        mpad_ref[2 * N_META:, :] = c_ref[0:N_META, :].astype(F32)
        outm_ref[...] = conv_rows(lambda d: mpad_ref[N_META + d:2 * N_META + d, :], N_META)


def _conv(c, cm, conv_w, conv_b, ln_g, ln_b, bsz, seq, tile):
    tps = seq // tile
    hpt = tile // CONV_HALO
    n_halo = bsz * seq // CONV_HALO

    def main_map(b, i):
        return (b * tps + i, 0)

    def prev_map(b, i):
        return (jnp.maximum((b * tps + i) * hpt - 1, 0), 0)

    def next_map(b, i):
        return (jnp.minimum((b * tps + i + 1) * hpt, n_halo - 1), 0)

    halo = (CONV_HALO, CONV_DIM)
    vec = _const_spec((1, CONV_DIM))
    return pl.pallas_call(
        _conv_kernel,
        grid=(bsz, tps),
        in_specs=[pl.BlockSpec((tile, CONV_DIM), main_map), pl.BlockSpec(halo, prev_map),
                  pl.BlockSpec(halo, next_map), _const_spec((N_META, CONV_DIM)),
                  _const_spec((CONV_WIDTH + 1, CONV_DIM)), vec, vec, vec],
        out_specs=[pl.BlockSpec((tile, CONV_DIM), main_map),
                   pl.BlockSpec((N_META, CONV_DIM), lambda b, i: (b, 0))],
        out_shape=[jax.ShapeDtypeStruct((bsz * seq, CONV_DIM), BF16),
                   jax.ShapeDtypeStruct((bsz * N_META, CONV_DIM), BF16)],
        scratch_shapes=[pltpu.VMEM((SUBLANES, tile + 2 * CONV_HALO, CONV_DIM), F32),
                        pltpu.VMEM((3 * N_META, CONV_DIM), F32)],
        compiler_params=_params(("arbitrary", "arbitrary")),
        name="conv_ln_silu",
    )(c, c, c, cm, conv_w, conv_b, ln_g, ln_b)


def _out_ffn_kernel(x_ref, a_ref, c_ref, wa_ref, wc_ref, g_ref, w1_ref, w3_ref, w2_ref, o_ref):
    x = (x_ref[...]
         + jnp.dot(a_ref[...], wa_ref[...], preferred_element_type=F32)
         + jnp.dot(c_ref[...], wc_ref[...], preferred_element_type=F32))
    n = _rms(x, g_ref[...]).astype(BF16)
    h1 = jnp.dot(n, w1_ref[...], preferred_element_type=F32)
    h3 = jnp.dot(n, w3_ref[...], preferred_element_type=F32)
    act = (h1 * jax.nn.sigmoid(h1) * h3).astype(BF16)
    o_ref[...] = x + jnp.dot(act, w2_ref[...], preferred_element_type=F32)


def _out_ffn(x, attn, cact, w_attn, w_conv, g, w1, w3, w2, tm):
    n = x.shape[0]
    wspec = _const_spec((ATTN_DIM, D_MODEL))
    return pl.pallas_call(
        _out_ffn_kernel,
        grid=(n // tm,),
        in_specs=[_row_spec(tm, D_MODEL), _row_spec(tm, ATTN_DIM), _row_spec(tm, CONV_DIM), wspec, wspec,
                  _const_spec((1, D_MODEL)), _const_spec((D_MODEL, D_FF)), _const_spec((D_MODEL, D_FF)),
                  _const_spec((D_FF, D_MODEL))],
        out_specs=_row_spec(tm, D_MODEL),
        out_shape=jax.ShapeDtypeStruct((n, D_MODEL), F32),
        compiler_params=_params(("arbitrary",)),
        name="out_proj_swiglu",
    )(x, attn, cact, w_attn, w_conv, g, w1, w3, w2)


def _pool_router_kernel(x_ref, prev_ref, next_ref, xm_ref, g_ref, wp_ref, ps_ref, g2_ref, rhi_ref, rlo_ref,
                        x3_ref, hn_ref, gate_ref, pos_ref, pos_t_ref, cnt_ref, npad_ref, *sum_refs, seq):
    tile = x_ref.shape[0]
    i = pl.program_id(1)
    first = i == 0
    last = i == pl.num_programs(1) - 1
    gain = g_ref[...]
    x = x_ref[...]
    n_main = _rms(x, gain)
    npad_ref[0:POOL_HALO, :] = _rms(jnp.where(first, xm_ref[...], prev_ref[...]), gain)
    npad_ref[POOL_HALO:POOL_HALO + tile, :] = n_main
    npad_ref[POOL_HALO + tile:, :] = jnp.where(last, 0.0, _rms(next_ref[...], gain))

    def window_sum(cols, w):
        first = POOL_HALO - w // 2
        needs, need, m = [], first + tile, w // 2
        while m >= 1:
            need += m
            needs.append((m, need))
            m //= 2
        needs.reverse()
        read = lambda lo, n: npad_ref[lo:lo + n, cols]
        for level, ((m, _), (_, n_next)) in enumerate(zip(needs, needs[1:])):
            buf = sum_refs[level % 2]
            buf[0:n_next, :] = read(0, n_next) + read(m, n_next)
            read = lambda lo, n, buf=buf: buf[lo:lo + n, :]
        return read(first, tile) + read(first + w // 2, tile)

    tok = i * tile + lax.broadcasted_iota(jnp.int32, (tile, 1), 0)
    mixed = []
    for gi, w in enumerate(POOL_WINDOWS):
        cols = slice(gi * POOL_GROUP_DIM, (gi + 1) * POOL_GROUP_DIM)
        half = w // 2
        acc = window_sum(cols, w)
        count = (w - jnp.maximum(tok + half - seq, 0)).astype(F32)
        diff = (acc / count - n_main[:, cols]).astype(BF16)
        mixed.append(jnp.dot(diff, wp_ref[gi], preferred_element_type=F32))
    x3 = x + jnp.concatenate(mixed, axis=-1) * ps_ref[...]
    x3_ref[...] = x3

    hn = _rms(x3, g2_ref[...])
    hn_ref[...] = hn
    h_hi = hn.astype(BF16)
    h_lo = (hn - h_hi.astype(F32)).astype(BF16)
    def logits_of(rows):
        return (jnp.dot(h_hi[rows], rhi_ref[...], preferred_element_type=F32)
                + jnp.dot(h_lo[rows], rhi_ref[...], preferred_element_type=F32)
                + jnp.dot(h_hi[rows], rlo_ref[...], preferred_element_type=F32))

    logits = jnp.concatenate([logits_of(slice(0, tile // 2)), logits_of(slice(tile // 2, tile))],
                             axis=0)
    lane = lax.broadcasted_iota(jnp.int32, logits.shape, 1)
    lane_f = lane.astype(F32)
    logits = jnp.where(lane < N_EXPERTS, logits, NEG_INF)
    m1 = jnp.max(logits, axis=-1, keepdims=True)
    i1 = jnp.min(jnp.where(logits == m1, lane_f, 256.0), axis=-1, keepdims=True)
    rest = jnp.where(lane_f == i1, NEG_INF, logits)
    m2 = jnp.max(rest, axis=-1, keepdims=True)
    i2 = jnp.min(jnp.where(rest == m2, lane_f, 256.0), axis=-1, keepdims=True)
    e2 = jnp.exp(m2 - m1)
    g1 = 1.0 / (1.0 + e2)
    gate_ref[...] = jnp.where(lane == 0, g1, jnp.where(lane == 1, e2 * g1, 0.0))

    hot = [jnp.where(lane_f == i1, 1.0, 0.0), jnp.where(lane_f == i2, 1.0, 0.0)]
    earlier = jnp.where(lax.broadcasted_iota(jnp.int32, (tile, tile), 1)
                        < lax.broadcasted_iota(jnp.int32, (tile, tile), 0), 1.0, 0.0).astype(BF16)
    before = [jnp.dot(earlier, h.astype(BF16), preferred_element_type=F32) for h in hot]
    count = [jnp.sum(h, axis=0, keepdims=True) for h in hot]
    total = count[0] + count[1]
    chunks = jnp.floor((total + (MOE_CHUNK - 1)) * (1.0 / MOE_CHUNK))
    lower_expert = jnp.where(lax.broadcasted_iota(jnp.int32, (128, 128), 0)
                             < lax.broadcasted_iota(jnp.int32, (128, 128), 1), 1.0, 0.0).astype(BF16)
    seg_off = MOE_CHUNK * jnp.dot(jnp.broadcast_to(chunks, (SUBLANES, 128)).astype(BF16), lower_expert,
                                  preferred_element_type=F32)[0:1, :]
    pos0 = jnp.sum(hot[0] * (before[0] + seg_off), axis=-1, keepdims=True)
    pos1 = jnp.sum(hot[1] * (before[1] + count[0] + seg_off), axis=-1, keepdims=True)
    pos = jnp.where(lane == 0, pos0, jnp.where(lane == 1, pos1, 0.0))
    pos_ref[...] = pos.astype(jnp.int32)
    pos_t_ref[...] = jnp.transpose(pos)[0:SUBLANES, :].astype(jnp.int32)
    cnt_ref[...] = jnp.broadcast_to(total, (SUBLANES, 128)).astype(jnp.int32)


def _pool_router(x2, x2m, g, w_pool, pool_scale, g2, r_hi, r_lo, bsz, seq, tile):
    tps = seq // tile
    hpt = tile // POOL_HALO
    n_halo = bsz * seq // POOL_HALO

    def main_map(b, i):
        return (b * tps + i, 0)

    def prev_map(b, i):
        return (jnp.maximum((b * tps + i) * hpt - 1, 0), 0)

    def next_map(b, i):
        return (jnp.minimum((b * tps + i + 1) * hpt, n_halo - 1), 0)

    halo = (POOL_HALO, D_MODEL)
    vec = _const_spec((1, D_MODEL))
    main = pl.BlockSpec((tile, D_MODEL), main_map)
    small = pl.BlockSpec((tile, 128), main_map)
    n = bsz * seq
    return pl.pallas_call(
        functools.partial(_pool_router_kernel, seq=seq),
        grid=(bsz, tps),
        in_specs=[main, pl.BlockSpec(halo, prev_map), pl.BlockSpec(halo, next_map),
                  pl.BlockSpec(halo, lambda b, i: (2 * b + 1, 0)), vec,
                  _const_spec((len(POOL_WINDOWS), POOL_GROUP_DIM, POOL_GROUP_DIM)), vec, vec,
                  _const_spec((D_MODEL, 128)), _const_spec((D_MODEL, 128))],
        out_specs=[main, main, small, small,
                   pl.BlockSpec((SUBLANES, tile), lambda b, i: (0, b * tps + i)),
                   pl.BlockSpec((None, SUBLANES, 128), lambda b, i: (b * tps + i, 0, 0))],
        out_shape=[jax.ShapeDtypeStruct((n, D_MODEL), F32), jax.ShapeDtypeStruct((n, D_MODEL), F32),
                   jax.ShapeDtypeStruct((n, 128), F32), jax.ShapeDtypeStruct((n, 128), jnp.int32),
                   jax.ShapeDtypeStruct((SUBLANES, n), jnp.int32),
                   jax.ShapeDtypeStruct((n // tile, SUBLANES, 128), jnp.int32)],
        scratch_shapes=[pltpu.VMEM((tile + 2 * POOL_HALO, D_MODEL), F32)]
                       + [pltpu.VMEM((tile + 2 * POOL_HALO, POOL_GROUP_DIM), F32)] * 2,
        compiler_params=_params(("arbitrary", "arbitrary")),
        name="pool_router",
    )(x2, x2, x2, x2m, g, w_pool, pool_scale, g2, r_hi, r_lo)


def _split_chunks(n_chunks):
    per_big = MOE_BIG // MOE_CHUNK
    n_big = lax.div(n_chunks, per_big)
    return n_big, n_chunks - n_big * per_big


def _start_pieces(copy, first_a, first_b, n_chunks):
    n_big, n_small = _split_chunks(n_chunks)

    def start(rows, base_a, base_b):
        def body(c, carry):
            copy(pl.multiple_of(base_a + c * rows, MOE_CHUNK), pl.multiple_of(base_b + c * rows, MOE_CHUNK),
                 rows).start()
            return carry
        return body

    lax.fori_loop(0, n_big, start(MOE_BIG, first_a, first_b), 0)
    lax.fori_loop(0, n_small, start(MOE_CHUNK, first_a + n_big * MOE_BIG, first_b + n_big * MOE_BIG), 0)


def _wait_pieces(copy, chunk_counts):
    splits = [_split_chunks(n) for n in chunk_counts]
    for rows, count in ((MOE_BIG, sum(s[0] for s in splits)), (MOE_CHUNK, sum(s[1] for s in splits))):
        def wait(c, carry, rows=rows):
            copy(0, 0, rows).wait()
            return carry

        lax.fori_loop(0, count, wait, 0)


def _start_segments(copy, off_ref, slot_ref, nch_ref, t):
    for e in range(N_EXPERTS):
        _start_pieces(copy, off_ref[t * N_EXPERTS + e], slot_ref[t * N_EXPERTS + e], nch_ref[t * N_EXPERTS + e])


def _wait_segments(copy, nch_ref, t):
    _wait_pieces(copy, [nch_ref[t * N_EXPERTS + e] for e in range(N_EXPERTS)])


def _dispatch_kernel(off_ref, slot_ref, nch_ref, tail_ref, ntail_ref, pos_ref, *refs, group_tiles):
    n_groups = len(group_tiles) - 1
    hn_refs = refs[:n_groups]
    xs_ref, buf_ref, zero_ref, sem = refs[n_groups:]
    t = pl.program_id(0)
    cur = lax.rem(t, 2)

    def copy_from(b):
        def copy(buf_row, slot_row, rows):
            return pltpu.make_async_copy(buf_ref.at[b, pl.ds(buf_row, rows)], xs_ref.at[pl.ds(slot_row, rows)],
                                         sem.at[b])
        return copy

    @pl.when(t >= 2)
    def _():
        _wait_segments(copy_from(cur), nch_ref, t - 2)

    rows = lax.broadcasted_iota(jnp.int32, (MOE_ROWS, MOE_TILE), 0)
    perm = jnp.where(rows == pos_ref[0:1, :], 1.0, jnp.where(rows == pos_ref[1:2, :], 1.0, 0.0)).astype(BF16)
    for g in range(n_groups):
        @pl.when((t >= group_tiles[g]) & (t < group_tiles[g + 1]))
        def _(g=g):
            buf_ref[cur] = jnp.dot(perm, hn_refs[g][...].astype(BF16), preferred_element_type=F32)

    _start_segments(copy_from(cur), off_ref, slot_ref, nch_ref, t)

    @pl.when(t == pl.num_programs(0) - 1)
    def _():
        @pl.when(t >= 1)
        def _():
            _wait_segments(copy_from(1 - cur), nch_ref, t - 1)

        _wait_segments(copy_from(cur), nch_ref, t)

        zero_ref[...] = jnp.zeros(zero_ref.shape, F32)

        def fill(unused_row, slot_row, rows):
            del unused_row
            return pltpu.make_async_copy(zero_ref.at[pl.ds(0, rows)], xs_ref.at[pl.ds(slot_row, rows)], sem.at[0])

        for e in range(N_EXPERTS):
            _start_pieces(fill, 0, tail_ref[e], ntail_ref[e])
        _wait_pieces(fill, [ntail_ref[e] for e in range(N_EXPERTS)])


def _dispatch(seg_tables, tail_tables, pos_t, hns, n_slots):
    group_tiles = [0]
    for hn in hns:
        group_tiles.append(group_tiles[-1] + hn.shape[0] // MOE_TILE)

    def hn_spec(g):
        first, count = group_tiles[g], group_tiles[g + 1] - group_tiles[g]
        return pl.BlockSpec((MOE_TILE, D_MODEL), lambda i, *_: (jnp.clip(i - first, 0, count - 1), 0))

    grid_spec = pltpu.PrefetchScalarGridSpec(
        num_scalar_prefetch=5,
        grid=(group_tiles[-1],),
        in_specs=[pl.BlockSpec((SUBLANES, MOE_TILE), lambda i, *_: (0, i))] + [hn_spec(g) for g in range(len(hns))],
        out_specs=pl.BlockSpec(memory_space=pl.ANY),
        scratch_shapes=[pltpu.VMEM((2, MOE_ROWS, D_MODEL), F32), pltpu.VMEM((MOE_BIG, D_MODEL), F32),
                        pltpu.SemaphoreType.DMA((2,))],
    )
    return pl.pallas_call(
        functools.partial(_dispatch_kernel, group_tiles=tuple(group_tiles)),
        grid_spec=grid_spec,
        out_shape=jax.ShapeDtypeStruct((n_slots, D_MODEL), F32),
        compiler_params=_params(("arbitrary",)),
        name="moe_dispatch",
    )(*seg_tables, *tail_tables, pos_t, *hns)


def _expert_kernel(te_ref, tv_ref, x_ref, w1_ref, w3_ref, w2_ref, y_ref, act_ref):
    del te_ref
    valid = tv_ref[pl.program_id(0)] == 1
    n_chunks = act_ref.shape[0]

    @pl.when(valid)
    def _():
        xb = x_ref[...].astype(BF16)
        for c in range(n_chunks):
            cols = slice(c * EXPERT_FF_CHUNK, (c + 1) * EXPERT_FF_CHUNK)
            h1 = jnp.dot(xb, w1_ref[:, cols], preferred_element_type=F32)
            h3 = jnp.dot(xb, w3_ref[:, cols], preferred_element_type=F32)
            act_ref[c] = (h1 * jax.nn.sigmoid(h1) * h3).astype(BF16)
        acc = jnp.dot(act_ref[0], w2_ref[0:EXPERT_FF_CHUNK, :], preferred_element_type=F32)
        for c in range(1, n_chunks):
            acc = acc + jnp.dot(act_ref[c], w2_ref[c * EXPERT_FF_CHUNK:(c + 1) * EXPERT_FF_CHUNK, :],
                                preferred_element_type=F32)
        y_ref[...] = acc

    @pl.when(jnp.logical_not(valid))
    def _():
        y_ref[...] = jnp.zeros(y_ref.shape, F32)


def _experts(tile_expert, tile_valid, xs, w1, w3, w2):
    n_slots = xs.shape[0]
    per_expert = dict(pipeline_mode=pl.Buffered(1))
    grid_spec = pltpu.PrefetchScalarGridSpec(
        num_scalar_prefetch=2,
        grid=(n_slots // EXPERT_TILE,),
        in_specs=[pl.BlockSpec((EXPERT_TILE, D_MODEL), lambda j, te, tv: (jnp.where(tv[j] == 1, j, 0), 0)),
                  pl.BlockSpec((None, D_MODEL, D_FF_EXPERT), lambda j, te, tv: (te[j], 0, 0), **per_expert),
                  pl.BlockSpec((None, D_MODEL, D_FF_EXPERT), lambda j, te, tv: (te[j], 0, 0), **per_expert),
                  pl.BlockSpec((None, D_FF_EXPERT, D_MODEL), lambda j, te, tv: (te[j], 0, 0), **per_expert)],
        out_specs=pl.BlockSpec((EXPERT_TILE, D_MODEL), lambda j, te, tv: (j, 0)),
        scratch_shapes=[pltpu.VMEM((D_FF_EXPERT // EXPERT_FF_CHUNK, EXPERT_TILE, EXPERT_FF_CHUNK), BF16)],
    )
    return pl.pallas_call(
        _expert_kernel,
        grid_spec=grid_spec,
        out_shape=jax.ShapeDtypeStruct((n_slots, D_MODEL), F32),
        compiler_params=_params(("arbitrary",)),
        name="moe_experts",
    )(tile_expert, tile_valid, xs, w1, w3, w2)


def _combine_kernel(off_ref, slot_ref, nch_ref, pos_ref, x_ref, gate_ref, y_ref, o_ref, buf_ref, sem):
    t = pl.program_id(0)
    cur = lax.rem(t, 2)

    def copy_into(b):
        def copy(buf_row, slot_row, rows):
            return pltpu.make_async_copy(y_ref.at[pl.ds(slot_row, rows)], buf_ref.at[b, pl.ds(buf_row, rows)],
                                         sem.at[b])
        return copy

    @pl.when(t == 0)
    def _():
        buf_ref[...] = jnp.zeros(buf_ref.shape, F32)
        _start_segments(copy_into(0), off_ref, slot_ref, nch_ref, 0)

    @pl.when(t + 1 < pl.num_programs(0))
    def _():
        _start_segments(copy_into(1 - cur), off_ref, slot_ref, nch_ref, t + 1)

    _wait_segments(copy_into(cur), nch_ref, t)
    yb = buf_ref[cur].astype(BF16)
    cols = lax.broadcasted_iota(jnp.int32, (MOE_TILE, MOE_ROWS), 1)
    pos = pos_ref[...]
    gates = gate_ref[...]
    out = x_ref[...]
    for k in range(TOP_K):
        pick = jnp.where(cols == pos[:, k:k + 1], 1.0, 0.0).astype(BF16)
        out = out + gates[:, k:k + 1] * jnp.dot(pick, yb, preferred_element_type=F32)
    o_ref[...] = out


def _combine(seg_tables, pos, x3, gates, y):
    n = x3.shape[0]
    grid_spec = pltpu.PrefetchScalarGridSpec(
        num_scalar_prefetch=3,
        grid=(n // MOE_TILE,),
        in_specs=[pl.BlockSpec((MOE_TILE, 128), lambda i, *_: (i, 0)),
                  pl.BlockSpec((MOE_TILE, D_MODEL), lambda i, *_: (i, 0)),
                  pl.BlockSpec((MOE_TILE, 128), lambda i, *_: (i, 0)),
                  pl.BlockSpec(memory_space=pl.ANY)],
        out_specs=pl.BlockSpec((MOE_TILE, D_MODEL), lambda i, *_: (i, 0)),
        scratch_shapes=[pltpu.VMEM((2, MOE_ROWS, D_MODEL), F32), pltpu.SemaphoreType.DMA((2,))],
    )
    return pl.pallas_call(
        _combine_kernel,
        grid_spec=grid_spec,
        out_shape=jax.ShapeDtypeStruct((n, D_MODEL), F32),
        compiler_params=_params(("arbitrary",)),
        name="moe_combine",
    )(*seg_tables, pos, x3, gates, y)


def _route(counts):
    n_tok_tiles = counts.shape[0]
    seg = (counts + MOE_CHUNK - 1) // MOE_CHUNK * MOE_CHUNK
    seg_off = jnp.cumsum(seg, axis=1) - seg
    used = jnp.sum(seg, axis=0)
    padded = (used + EXPERT_TILE - 1) // EXPERT_TILE * EXPERT_TILE
    pad_end = jnp.cumsum(padded)
    pad_start = pad_end - padded
    seg_slot = pad_start[None, :] + jnp.cumsum(seg, axis=0) - seg
    worst_rows = n_tok_tiles * (TOP_K * MOE_TILE + N_EXPERTS * (MOE_CHUNK - 1)) + N_EXPERTS * (EXPERT_TILE - 1)
    n_tiles = (worst_rows + EXPERT_TILE - 1) // EXPERT_TILE
    tile_start = jnp.arange(n_tiles, dtype=jnp.int32) * EXPERT_TILE
    tile_expert = jnp.sum((tile_start[:, None] >= pad_end[None, :]).astype(jnp.int32), axis=1)
    tile_expert = jnp.minimum(tile_expert, N_EXPERTS - 1)
    tile_valid = (tile_start < pad_end[-1]).astype(jnp.int32)
    segs = (seg_off, seg_slot, seg // MOE_CHUNK)
    tail_end = pad_end.at[N_EXPERTS - 1].set(n_tiles * EXPERT_TILE)
    tails = (pad_start + used, (tail_end - pad_start - used) // MOE_CHUNK)
    return segs, tails, tile_expert, tile_valid, n_tiles


def _trunk(groups, meta_tokens, norm_mix_e, w_in, q_gain, k_gain, rel_bias, conv_w, conv_b,
           conv_ln_g, conv_ln_b, w_out, norm_ffn_e, ffn_w1, ffn_w3, ffn_w2,
           norm_mix_o, pool_w, pool_scale, norm_ffn_o, router_w, moe_w1, moe_w3, moe_w2):
    row = lambda a: a.reshape(1, -1).astype(F32)
    shapes = [(g.shape[0], g.shape[1]) for g in groups]
    xs = [g.reshape(-1, D_MODEL) for g in groups]

    hsum = jnp.asarray(np.kron(np.eye(N_HEADS), np.ones((HEAD_DIM, HEAD_DIM))), BF16)
    qg = row(jnp.tile(q_gain[0], N_HEADS))
    kg = row(jnp.tile(k_gain[0], N_HEADS))
    w_in_b = w_in[0].astype(BF16)
    g_mix = row(norm_mix_e[0])
    bias_pat = _attn_bias_patterns(rel_bias[0])
    conv_w_p = jnp.concatenate([conv_w[0].astype(F32), jnp.zeros((1, CONV_DIM), F32)], axis=0)
    w_out_a = w_out[0, :ATTN_DIM].astype(BF16)
    w_out_c = w_out[0, ATTN_DIM:].astype(BF16)
    g_ffn = row(norm_ffn_e[0])
    w1b, w3b, w2b = ffn_w1[0].astype(BF16), ffn_w3[0].astype(BF16), ffn_w2[0].astype(BF16)

    meta = meta_tokens.astype(F32)
    qm, km, vm, cm = _proj(meta, g_mix, w_in_b, qg, kg, hsum, N_META)
    attn_m = _meta_attention(qm, km, vm)

    x2s, x2ms = [], []
    for x, (bsz, seq) in zip(xs, shapes):
        q, k, v, c = _proj(x, g_mix, w_in_b, qg, kg, hsum, TOKEN_TILE)
        attn = _attention(q, k, v, km, vm, bias_pat, bsz, seq)
        cact, cact_m = _conv(c, cm, conv_w_p, row(conv_b[0]), row(conv_ln_g[0]), row(conv_ln_b[0]),
                             bsz, seq, TOKEN_TILE)
        x2s.append(_out_ffn(x, attn, cact, w_out_a, w_out_c, g_ffn, w1b, w3b, w2b, TOKEN_TILE))
        x2ms.append(_out_ffn(jnp.tile(meta, (bsz, 1)), jnp.tile(attn_m, (bsz, 1)), cact_m, w_out_a, w_out_c,
                             g_ffn, w1b, w3b, w2b, bsz * N_META))

    g_pool = row(norm_mix_o[0])
    w_pool_b = pool_w[0].astype(BF16)
    g_moe = row(norm_ffn_o[0])
    rw = jnp.zeros((D_MODEL, 128), F32).at[:, :N_EXPERTS].set(router_w[0].astype(F32))
    r_hi = rw.astype(BF16)
    r_lo = (rw - r_hi.astype(F32)).astype(BF16)
    mw1, mw3, mw2 = moe_w1[0].astype(BF16), moe_w3[0].astype(BF16), moe_w2[0].astype(BF16)

    assert TOKEN_TILE == MOE_TILE
    routed = [_pool_router(x2, x2m, g_pool, w_pool_b, row(pool_scale[0]), g_moe, r_hi, r_lo, bsz, seq, MOE_TILE)
              for x2, x2m, (bsz, seq) in zip(x2s, x2ms, shapes)]
    counts = jnp.concatenate([r[5][:, 0, :N_EXPERTS] for r in routed], axis=0)
    segs, tails, tile_expert, tile_valid, n_tiles = _route(counts)
    pos_t = jnp.concatenate([r[4] for r in routed], axis=1)
    slots = _dispatch(tuple(a.reshape(-1) for a in segs), tails, pos_t, [r[1] for r in routed],
                      n_tiles * EXPERT_TILE)
    y = _experts(tile_expert, tile_valid, slots, mw1, mw3, mw2)
    outs, t0 = [], 0
    for (x3, _, gate, pos, _, cnt), (bsz, seq) in zip(routed, shapes):
        t1 = t0 + cnt.shape[0]
        seg_tables = tuple(a[t0:t1].reshape(-1) for a in segs)
        outs.append(_combine(seg_tables, pos, x3, gate, y).reshape(bsz, seq, D_MODEL))
        t0 = t1
    return tuple(outs)


def kernel(x_prompt, x_sample, meta_tokens, norm_mix_e, w_in, q_gain, k_gain, rel_bias, conv_w, conv_b, conv_ln_g, conv_ln_b, w_out, norm_ffn_e, ffn_w1, ffn_w3, ffn_w2, norm_mix_o, pool_w, pool_scale, norm_ffn_o, router_w, moe_w1, moe_w3, moe_w2):
    return _trunk([x_prompt, x_sample], meta_tokens, norm_mix_e, w_in, q_gain, k_gain, rel_bias, conv_w,
                  conv_b, conv_ln_g, conv_ln_b, w_out, norm_ffn_e, ffn_w1, ffn_w3, ffn_w2,
                  norm_mix_o, pool_w, pool_scale, norm_ffn_o, router_w, moe_w1, moe_w3, moe_w2)
```

```python
import functools

import numpy as np
import jax
import jax.numpy as jnp
from jax import lax
from jax.experimental import pallas as pl
from jax.experimental.pallas import tpu as pltpu

D_MODEL = 1024
N_META = 16
GRID_W = 64
N_HEADS = 8
HEAD_DIM = 64
ATTN_DIM = N_HEADS * HEAD_DIM
CONV_DIM = D_MODEL - ATTN_DIM
IN_DIM = 3 * ATTN_DIM + 2 * CONV_DIM
WIN_ROWS = 8
WIN_COLS = 16
CONV_WIDTH = 31
CONV_HALF = CONV_WIDTH // 2
POOL_WINDOWS = (2, 4, 8, 16)
POOL_GROUP_DIM = D_MODEL // len(POOL_WINDOWS)
D_FF = 2816
N_EXPERTS = 8
TOP_K = 2
D_FF_EXPERT = 3584
EPS = 1e-6

F32 = jnp.float32
BF16 = jnp.bfloat16
SUBLANES = 8

TOKEN_TILE = 512
ATTN_ROWS = 4
ATTN_Q = ATTN_ROWS * GRID_W
ATTN_KROWS = ATTN_ROWS + WIN_ROWS
ATTN_KBLK = ATTN_KROWS * GRID_W // ATTN_Q
CONV_HALO = 16
CONV_CHUNK = 64
POOL_HALO = 8
EXPERT_TILE = 1024
EXPERT_FF_CHUNK = 512
MOE_TILE = 512
MOE_CHUNK = 8
MOE_BIG = 32
MOE_ROWS = 1152
MOE_ROW_BLOCK = 384
SLOT_W = D_MODEL + 128
NEG_INF = -1e30
VMEM_LIMIT = 56 * 1024 * 1024


def _params(sem):
    return pltpu.CompilerParams(dimension_semantics=sem, vmem_limit_bytes=VMEM_LIMIT)


def _const_spec(shape):
    zeros = (0,) * len(shape)
    return pl.BlockSpec(shape, lambda *_: zeros)


def _row_spec(tm, cols):
    return pl.BlockSpec((tm, cols), lambda i: (i, 0))


def _rms(x, gain):
    ms = jnp.mean(x * x, axis=-1, keepdims=True)
    return x * lax.rsqrt(ms + EPS) * gain


def _proj_kernel(x_ref, g_ref, w_ref, qg_ref, kg_ref, hsum_ref, q_ref, k_ref, v_ref, c_ref):
    n = _rms(x_ref[...], g_ref[...]).astype(BF16)
    proj = jnp.dot(n, w_ref[...], preferred_element_type=F32)
    q = proj[:, :ATTN_DIM]
    k = proj[:, ATTN_DIM:2 * ATTN_DIM]
    v = proj[:, 2 * ATTN_DIM:3 * ATTN_DIM]
    u = proj[:, 3 * ATTN_DIM:3 * ATTN_DIM + CONV_DIM]
    g = proj[:, 3 * ATTN_DIM + CONV_DIM:]

    def head_norm(a, gain):
        ss = jnp.dot((a * a).astype(BF16), hsum_ref[...], preferred_element_type=F32)
        return a * lax.rsqrt(ss * (1.0 / HEAD_DIM) + EPS) * gain

    q_ref[...] = (head_norm(q, qg_ref[...]) * (HEAD_DIM ** -0.5)).astype(BF16)
    k_ref[...] = head_norm(k, kg_ref[...]).astype(BF16)
    v_ref[...] = v.astype(BF16)
    c_ref[...] = (u * jax.nn.sigmoid(g)).astype(BF16)


def _proj(x, norm_g, w_in, qg, kg, hsum, tm):
    n = x.shape[0]
    out = jax.ShapeDtypeStruct((n, ATTN_DIM), BF16)
    return pl.pallas_call(
        _proj_kernel,
        grid=(n // tm,),
        in_specs=[_row_spec(tm, D_MODEL), _const_spec((1, D_MODEL)), _const_spec((D_MODEL, IN_DIM)),
                  _const_spec((1, ATTN_DIM)), _const_spec((1, ATTN_DIM)), _const_spec((ATTN_DIM, ATTN_DIM))],
        out_specs=[_row_spec(tm, ATTN_DIM)] * 4,
        out_shape=[out] * 4,
        compiler_params=_params(("arbitrary",)),
        name="in_proj",
    )(x, norm_g, w_in, qg, kg, hsum)


def _attn_bias_patterns(rel_bias):
    n_dr, n_dc = 2 * WIN_ROWS - 1, 2 * WIN_COLS - 1
    qr = np.arange(ATTN_ROWS)[:, None]
    kr = np.arange(ATTN_KROWS)[None, :]
    qc = np.arange(GRID_W)[:, None]
    kc = np.arange(GRID_W)[None, :]
    cs = np.clip(qc - WIN_COLS // 2, 0, GRID_W - WIN_COLS)
    col_ok = (kc >= cs) & (kc < cs + WIN_COLS)
    dc = np.clip(kc - qc + (WIN_COLS - 1), 0, n_dc - 1)
    col_sel = (dc[None] == np.arange(n_dc)[:, None, None]).astype(np.float32)
    row_sel, ok = [], []
    for off, rs in ((0, 0 * qr), (ATTN_ROWS, qr), (2 * ATTN_ROWS, ATTN_ROWS + 0 * qr)):
        row_ok = (kr >= rs) & (kr < rs + WIN_ROWS)
        dr = np.clip(kr - (off + qr) + (WIN_ROWS - 1), 0, n_dr - 1)
        row_sel.append((dr[..., None] == np.arange(n_dr)).astype(np.float32))
        ok.append(row_ok[:, None, :, None] & col_ok[None, :, None, :])
    row_sel = np.stack(row_sel)
    ok = np.stack(ok).reshape(3, 1, ATTN_Q, ATTN_KROWS * GRID_W)
    vals = jnp.einsum('pqka,hab,bcd->phqckd', row_sel, rel_bias.astype(F32), col_sel,
                      precision=lax.Precision.HIGHEST)
    vals = vals.reshape(3, N_HEADS, ATTN_Q, ATTN_KROWS * GRID_W)
    return jnp.where(ok, vals, NEG_INF)


def _attn_kernel(q_ref, k0_ref, k1_ref, k2_ref, v0_ref, v1_ref, v2_ref, km_ref, vm_ref, bias_ref, o_ref):
    nt = (((1,), (1,)), ((), ()))
    k_refs = (k0_ref, k1_ref, k2_ref)
    v_refs = (v0_ref, v1_ref, v2_ref)
    outs = []
    for h in range(N_HEADS):
        sl = slice(h * HEAD_DIM, (h + 1) * HEAD_DIM)
        qh = q_ref[:, sl]
        s_meta = lax.dot_general(qh, km_ref[:, sl], nt, preferred_element_type=F32)
        s_loc = [lax.dot_general(qh, k_refs[t][:, sl], nt, preferred_element_type=F32)
                 + bias_ref[0, h, :, t * ATTN_Q:(t + 1) * ATTN_Q] for t in range(ATTN_KBLK)]
        m_loc = functools.reduce(jnp.maximum, s_loc)
        m = jnp.maximum(jnp.max(s_meta, axis=-1, keepdims=True), jnp.max(m_loc, axis=-1, keepdims=True))
        p_meta = jnp.exp(s_meta - m)
        acc = jnp.dot(p_meta.astype(BF16), vm_ref[:, sl], preferred_element_type=F32)
        p_sum = None
        for t in range(ATTN_KBLK):
            p = jnp.exp(s_loc[t] - m)
            p_sum = p if p_sum is None else p_sum + p
            acc = acc + jnp.dot(p.astype(BF16), v_refs[t][:, sl], preferred_element_type=F32)
        denom = jnp.sum(p_meta, axis=-1, keepdims=True) + jnp.sum(p_sum, axis=-1, keepdims=True)
        outs.append(acc / denom)
    o_ref[...] = jnp.concatenate(outs, axis=-1).astype(BF16)


def _attention(q, k, v, km, vm, bias_pat, bsz, seq):
    nb = seq // ATTN_Q
    assert seq % ATTN_Q == 0 and nb >= ATTN_KBLK

    def q_map(b, j):
        return (b * nb + j, 0)

    def kv_map(t):
        return lambda b, j: (b * nb + jnp.clip(j - 1, 0, nb - ATTN_KBLK) + t, 0)

    def bias_map(b, j):
        return (jnp.where(j == 0, 0, jnp.where(j == nb - 1, 2, 1)), 0, 0, 0)

    blk = (ATTN_Q, ATTN_DIM)
    kv_specs = [pl.BlockSpec(blk, kv_map(t)) for t in range(ATTN_KBLK)]
    return pl.pallas_call(
        _attn_kernel,
        grid=(bsz, nb),
        in_specs=[pl.BlockSpec(blk, q_map)] + kv_specs + kv_specs
                 + [_const_spec((N_META, ATTN_DIM)), _const_spec((N_META, ATTN_DIM)),
                    pl.BlockSpec((1, N_HEADS, ATTN_Q, ATTN_KBLK * ATTN_Q), bias_map)],
        out_specs=pl.BlockSpec(blk, q_map),
        out_shape=jax.ShapeDtypeStruct((bsz * seq, ATTN_DIM), BF16),
        compiler_params=_params(("arbitrary", "arbitrary")),
        name="nbr_attn",
    )(q, k, k, k, v, v, v, km, vm, bias_pat)


def _meta_attn_kernel(q_ref, k_ref, v_ref, o_ref):
    nt = (((1,), (1,)), ((), ()))
    outs = []
    for h in range(N_HEADS):
        sl = slice(h * HEAD_DIM, (h + 1) * HEAD_DIM)
        s = lax.dot_general(q_ref[:, sl], k_ref[:, sl], nt, preferred_element_type=F32)
        p = jnp.exp(s - jnp.max(s, axis=-1, keepdims=True))
        o = jnp.dot(p.astype(BF16), v_ref[:, sl], preferred_element_type=F32)
        outs.append(o / jnp.sum(p, axis=-1, keepdims=True))
    o_ref[...] = jnp.concatenate(outs, axis=-1).astype(BF16)


def _meta_attention(qm, km, vm):
    spec = _const_spec((N_META, ATTN_DIM))
    return pl.pallas_call(
        _meta_attn_kernel,
        in_specs=[spec] * 3, out_specs=spec, grid=(1,),
        out_shape=jax.ShapeDtypeStruct((N_META, ATTN_DIM), BF16),
        compiler_params=_params(("arbitrary",)),
        name="meta_attn",
    )(qm, km, vm)


def _conv_kernel(c_ref, prev_ref, next_ref, cm_ref, w_ref, b_ref, lg_ref, lb_ref,
                 out_ref, outm_ref, xs_ref, mpad_ref):
    tile = c_ref.shape[0]
    n_pad = tile + 2 * CONV_HALO
    i = pl.program_id(1)
    first = i == 0
    last = i == pl.num_programs(1) - 1
    cm = cm_ref[...].astype(F32)
    xs_ref[0, 0:CONV_HALO, :] = jnp.where(first, cm, prev_ref[...].astype(F32))
    xs_ref[0, CONV_HALO:CONV_HALO + tile, :] = c_ref[...].astype(F32)
    xs_ref[0, CONV_HALO + tile:, :] = jnp.where(last, 0.0, next_ref[...].astype(F32))
    for s in range(1, SUBLANES):
        xs_ref[s, 0:n_pad - SUBLANES, :] = xs_ref[0, s:s + n_pad - SUBLANES, :]

    def conv_rows(read, rows):
        acc = jnp.zeros((rows, CONV_DIM), F32)
        for j in range(CONV_WIDTH):
            acc = acc + w_ref[j:j + 1, :] * read(j - CONV_HALF)
        y = acc + b_ref[...]
        mu = jnp.mean(y, axis=-1, keepdims=True)
        yc = y - mu
        var = jnp.mean(yc * yc, axis=-1, keepdims=True)
        z = yc * lax.rsqrt(var + EPS) * lg_ref[...] + lb_ref[...]
        return (z * jax.nn.sigmoid(z)).astype(BF16)

    for rc in range(tile // CONV_CHUNK):
        start = CONV_HALO + rc * CONV_CHUNK

        def read(d, start=start):
            lo = start + d
            return xs_ref[lo % SUBLANES, lo - lo % SUBLANES:lo - lo % SUBLANES + CONV_CHUNK, :]

        out_ref[rc * CONV_CHUNK:(rc + 1) * CONV_CHUNK, :] = conv_rows(read, CONV_CHUNK)

    @pl.when(first)
    def _():
        mpad_ref[0:N_META, :] = jnp.zeros((N_META, CONV_DIM), F32)
        mpad_ref[N_META:2 * N_META, :] = cm
        mpad_ref[2 * N_META:, :] = c_ref[0:N_META, :].astype(F32)
        outm_ref[...] = conv_rows(lambda d: mpad_ref[N_META + d:2 * N_META + d, :], N_META)


def _conv(c, cm, conv_w, conv_b, ln_g, ln_b, bsz, seq, tile):
    tps = seq // tile
    hpt = tile // CONV_HALO
    n_halo = bsz * seq // CONV_HALO

    def main_map(b, i):
        return (b * tps + i, 0)

    def prev_map(b, i):
        return (jnp.maximum((b * tps + i) * hpt - 1, 0), 0)

    def next_map(b, i):
        return (jnp.minimum((b * tps + i + 1) * hpt, n_halo - 1), 0)

    halo = (CONV_HALO, CONV_DIM)
    vec = _const_spec((1, CONV_DIM))
    return pl.pallas_call(
        _conv_kernel,
        grid=(bsz, tps),
        in_specs=[pl.BlockSpec((tile, CONV_DIM), main_map), pl.BlockSpec(halo, prev_map),
                  pl.BlockSpec(halo, next_map), _const_spec((N_META, CONV_DIM)),
                  _const_spec((CONV_WIDTH + 1, CONV_DIM)), vec, vec, vec],
        out_specs=[pl.BlockSpec((tile, CONV_DIM), main_map),
                   pl.BlockSpec((N_META, CONV_DIM), lambda b, i: (b, 0))],
        out_shape=[jax.ShapeDtypeStruct((bsz * seq, CONV_DIM), BF16),
                   jax.ShapeDtypeStruct((bsz * N_META, CONV_DIM), BF16)],
        scratch_shapes=[pltpu.VMEM((SUBLANES, tile + 2 * CONV_HALO, CONV_DIM), F32),
                        pltpu.VMEM((3 * N_META, CONV_DIM), F32)],
        compiler_params=_params(("arbitrary", "arbitrary")),
        name="conv_ln_silu",
    )(c, c, c, cm, conv_w, conv_b, ln_g, ln_b)


def _out_ffn_kernel(x_ref, a_ref, c_ref, wa_ref, wc_ref, g_ref, w1_ref, w3_ref, w2_ref, o_ref):
    x = (x_ref[...]
         + jnp.dot(a_ref[...], wa_ref[...], preferred_element_type=F32)
         + jnp.dot(c_ref[...], wc_ref[...], preferred_element_type=F32))
    n = _rms(x, g_ref[...]).astype(BF16)
    h1 = jnp.dot(n, w1_ref[...], preferred_element_type=F32)
    h3 = jnp.dot(n, w3_ref[...], preferred_element_type=F32)
    act = (h1 * jax.nn.sigmoid(h1) * h3).astype(BF16)
    o_ref[...] = x + jnp.dot(act, w2_ref[...], preferred_element_type=F32)


def _out_ffn(x, attn, cact, w_attn, w_conv, g, w1, w3, w2, tm):
    n = x.shape[0]
    wspec = _const_spec((ATTN_DIM, D_MODEL))
    return pl.pallas_call(
        _out_ffn_kernel,
        grid=(n // tm,),
        in_specs=[_row_spec(tm, D_MODEL), _row_spec(tm, ATTN_DIM), _row_spec(tm, CONV_DIM), wspec, wspec,
                  _const_spec((1, D_MODEL)), _const_spec((D_MODEL, D_FF)), _const_spec((D_MODEL, D_FF)),
                  _const_spec((D_FF, D_MODEL))],
        out_specs=_row_spec(tm, D_MODEL),
        out_shape=jax.ShapeDtypeStruct((n, D_MODEL), F32),
        compiler_params=_params(("arbitrary",)),
        name="out_proj_swiglu",
    )(x, attn, cact, w_attn, w_conv, g, w1, w3, w2)


def _pool_router_kernel(x_ref, prev_ref, next_ref, xm_ref, g_ref, wp_ref, ps_ref, g2_ref, rhi_ref, rlo_ref,
                        x3_ref, hn_ref, gate_ref, pos_ref, pos_t_ref, cnt_ref, npad_ref, *sum_refs, seq):
    tile = x_ref.shape[0]
    i = pl.program_id(1)
    first = i == 0
    last = i == pl.num_programs(1) - 1
    gain = g_ref[...]
    x = x_ref[...]
    n_main = _rms(x, gain)
    npad_ref[0:POOL_HALO, :] = _rms(jnp.where(first, xm_ref[...], prev_ref[...]), gain)
    npad_ref[POOL_HALO:POOL_HALO + tile, :] = n_main
    npad_ref[POOL_HALO + tile:, :] = jnp.where(last, 0.0, _rms(next_ref[...], gain))

    def window_sum(cols, w):
        first = POOL_HALO - w // 2
        needs, need, m = [], first + tile, w // 2
        while m >= 1:
            need += m
            needs.append((m, need))
            m //= 2
        needs.reverse()
        read = lambda lo, n: npad_ref[lo:lo + n, cols]
        for level, ((m, _), (_, n_next)) in enumerate(zip(needs, needs[1:])):
            buf = sum_refs[level % 2]
            buf[0:n_next, :] = read(0, n_next) + read(m, n_next)
            read = lambda lo, n, buf=buf: buf[lo:lo + n, :]
        return read(first, tile) + read(first + w // 2, tile)

    tok = i * tile + lax.broadcasted_iota(jnp.int32, (tile, 1), 0)
    mixed = []
    for gi, w in enumerate(POOL_WINDOWS):
        cols = slice(gi * POOL_GROUP_DIM, (gi + 1) * POOL_GROUP_DIM)
        half = w // 2
        acc = window_sum(cols, w)
        count = (w - jnp.maximum(tok + half - seq, 0)).astype(F32)
        diff = (acc / count - n_main[:, cols]).astype(BF16)
        mixed.append(jnp.dot(diff, wp_ref[gi], preferred_element_type=F32))
    x3 = x + jnp.concatenate(mixed, axis=-1) * ps_ref[...]
    x3_ref[...] = x3

    hn = _rms(x3, g2_ref[...])
    hn_ref[...] = hn
    h_hi = hn.astype(BF16)
    h_lo = (hn - h_hi.astype(F32)).astype(BF16)
    def logits_of(rows):
        return (jnp.dot(h_hi[rows], rhi_ref[...], preferred_element_type=F32)
                + jnp.dot(h_lo[rows], rhi_ref[...], preferred_element_type=F32)
                + jnp.dot(h_hi[rows], rlo_ref[...], preferred_element_type=F32))

    logits = jnp.concatenate([logits_of(slice(0, tile // 2)), logits_of(slice(tile // 2, tile))],
                             axis=0)
    lane = lax.broadcasted_iota(jnp.int32, logits.shape, 1)
    lane_f = lane.astype(F32)
    logits = jnp.where(lane < N_EXPERTS, logits, NEG_INF)
    m1 = jnp.max(logits, axis=-1, keepdims=True)
    i1 = jnp.min(jnp.where(logits == m1, lane_f, 256.0), axis=-1, keepdims=True)
    rest = jnp.where(lane_f == i1, NEG_INF, logits)
    m2 = jnp.max(rest, axis=-1, keepdims=True)
    i2 = jnp.min(jnp.where(rest == m2, lane_f, 256.0), axis=-1, keepdims=True)
    e2 = jnp.exp(m2 - m1)
    g1 = 1.0 / (1.0 + e2)
    gate_ref[...] = jnp.where(lane == 0, g1, jnp.where(lane == 1, e2 * g1, 0.0))

    hot = [jnp.where(lane_f == i1, 1.0, 0.0), jnp.where(lane_f == i2, 1.0, 0.0)]
    earlier = jnp.where(lax.broadcasted_iota(jnp.int32, (tile, tile), 1)
                        < lax.broadcasted_iota(jnp.int32, (tile, tile), 0), 1.0, 0.0).astype(BF16)
    before = [jnp.dot(earlier, h.astype(BF16), preferred_element_type=F32) for h in hot]
    count = [jnp.sum(h, axis=0, keepdims=True) for h in hot]
    total = count[0] + count[1]
    chunks = jnp.floor((total + (MOE_CHUNK - 1)) * (1.0 / MOE_CHUNK))
    lower_expert = jnp.where(lax.broadcasted_iota(jnp.int32, (128, 128), 0)
                             < lax.broadcasted_iota(jnp.int32, (128, 128), 1), 1.0, 0.0).astype(BF16)
    seg_off = MOE_CHUNK * jnp.dot(jnp.broadcast_to(chunks, (SUBLANES, 128)).astype(BF16), lower_expert,
                                  preferred_element_type=F32)[0:1, :]
    pos0 = jnp.sum(hot[0] * (before[0] + seg_off), axis=-1, keepdims=True)
    pos1 = jnp.sum(hot[1] * (before[1] + count[0] + seg_off), axis=-1, keepdims=True)
    pos = jnp.where(lane == 0, pos0, jnp.where(lane == 1, pos1, 0.0))
    pos_ref[...] = pos.astype(jnp.int32)
    pos_t_ref[...] = jnp.transpose(pos)[0:SUBLANES, :].astype(jnp.int32)
    cnt_ref[...] = jnp.broadcast_to(total, (SUBLANES, 128)).astype(jnp.int32)


def _pool_router(x2, x2m, g, w_pool, pool_scale, g2, r_hi, r_lo, bsz, seq, tile):
    tps = seq // tile
    hpt = tile // POOL_HALO
    n_halo = bsz * seq // POOL_HALO

    def main_map(b, i):
        return (b * tps + i, 0)

    def prev_map(b, i):
        return (jnp.maximum((b * tps + i) * hpt - 1, 0), 0)

    def next_map(b, i):
        return (jnp.minimum((b * tps + i + 1) * hpt, n_halo - 1), 0)

    halo = (POOL_HALO, D_MODEL)
    vec = _const_spec((1, D_MODEL))
    main = pl.BlockSpec((tile, D_MODEL), main_map)
    small = pl.BlockSpec((tile, 128), main_map)
    n = bsz * seq
    return pl.pallas_call(
        functools.partial(_pool_router_kernel, seq=seq),
        grid=(bsz, tps),
        in_specs=[main, pl.BlockSpec(halo, prev_map), pl.BlockSpec(halo, next_map),
                  pl.BlockSpec(halo, lambda b, i: (2 * b + 1, 0)), vec,
                  _const_spec((len(POOL_WINDOWS), POOL_GROUP_DIM, POOL_GROUP_DIM)), vec, vec,
                  _const_spec((D_MODEL, 128)), _const_spec((D_MODEL, 128))],
        out_specs=[main, main, small, small,
                   pl.BlockSpec((SUBLANES, tile), lambda b, i: (0, b * tps + i)),
                   pl.BlockSpec((None, SUBLANES, 128), lambda b, i: (b * tps + i, 0, 0))],
        out_shape=[jax.ShapeDtypeStruct((n, D_MODEL), F32), jax.ShapeDtypeStruct((n, D_MODEL), F32),
                   jax.ShapeDtypeStruct((n, 128), F32), jax.ShapeDtypeStruct((n, 128), jnp.int32),
                   jax.ShapeDtypeStruct((SUBLANES, n), jnp.int32),
                   jax.ShapeDtypeStruct((n // tile, SUBLANES, 128), jnp.int32)],
        scratch_shapes=[pltpu.VMEM((tile + 2 * POOL_HALO, D_MODEL), F32)]
                       + [pltpu.VMEM((tile + 2 * POOL_HALO, POOL_GROUP_DIM), F32)] * 2,
        compiler_params=_params(("arbitrary", "arbitrary")),
        name="pool_router",
    )(x2, x2, x2, x2m, g, w_pool, pool_scale, g2, r_hi, r_lo)


def _split_chunks(n_chunks):
    per_big = MOE_BIG // MOE_CHUNK
    n_big = lax.div(n_chunks, per_big)
    return n_big, n_chunks - n_big * per_big


def _start_pieces(copy, first_a, first_b, n_chunks):
    n_big, n_small = _split_chunks(n_chunks)

    def start(rows, base_a, base_b):
        def body(c, carry):
            copy(pl.multiple_of(base_a + c * rows, MOE_CHUNK), pl.multiple_of(base_b + c * rows, MOE_CHUNK),
                 rows).start()
            return carry
        return body

    lax.fori_loop(0, n_big, start(MOE_BIG, first_a, first_b), 0)
    lax.fori_loop(0, n_small, start(MOE_CHUNK, first_a + n_big * MOE_BIG, first_b + n_big * MOE_BIG), 0)


def _wait_pieces(copy, chunk_counts):
    splits = [_split_chunks(n) for n in chunk_counts]
    for rows, count in ((MOE_BIG, sum(s[0] for s in splits)), (MOE_CHUNK, sum(s[1] for s in splits))):
        def wait(c, carry, rows=rows):
            copy(0, 0, rows).wait()
            return carry

        lax.fori_loop(0, count, wait, 0)


def _start_segments(copy, off_ref, slot_ref, nch_ref, t):
    for e in range(N_EXPERTS):
        _start_pieces(copy, off_ref[t * N_EXPERTS + e], slot_ref[t * N_EXPERTS + e], nch_ref[t * N_EXPERTS + e])


def _wait_segments(copy, nch_ref, t):
    _wait_pieces(copy, [nch_ref[t * N_EXPERTS + e] for e in range(N_EXPERTS)])


def _dispatch_kernel(off_ref, slot_ref, nch_ref, tail_ref, ntail_ref, pos_ref, *refs, group_tiles):
    n_groups = len(group_tiles) - 1
    hn_refs, gate_refs = refs[:n_groups], refs[n_groups:2 * n_groups]
    xs_ref, buf_ref, zero_ref, sem = refs[2 * n_groups:]
    t = pl.program_id(0)
    cur = lax.rem(t, 2)

    def copy_from(b):
        def copy(buf_row, slot_row, rows):
            return pltpu.make_async_copy(buf_ref.at[b, pl.ds(buf_row, rows)], xs_ref.at[pl.ds(slot_row, rows)],
                                         sem.at[b])
        return copy

    @pl.when(t >= 2)
    def _():
        _wait_segments(copy_from(cur), nch_ref, t - 2)

    for g in range(n_groups):
        @pl.when((t >= group_tiles[g]) & (t < group_tiles[g + 1]))
        def _(g=g):
            hn = hn_refs[g][...].astype(BF16)
            gates = gate_refs[g][...]
            lane = lax.broadcasted_iota(jnp.int32, gates.shape, 1)
            pieces = []
            for k in range(TOP_K):
                gk = gates[:, k:k + 1]
                hi = gk.astype(BF16).astype(F32)
                mid = (gk - hi).astype(BF16).astype(F32)
                pieces.append(jnp.where(lane == 0, hi, jnp.where(lane == 1, mid, jnp.where(lane == 2, gk - hi - mid, 0.0)))
                              .astype(BF16))
            for r0 in range(0, MOE_ROWS, MOE_ROW_BLOCK):
                rows = r0 + lax.broadcasted_iota(jnp.int32, (MOE_ROW_BLOCK, MOE_TILE), 0)
                own = [jnp.where(rows == pos_ref[k:k + 1, :], 1.0, 0.0).astype(BF16) for k in range(TOP_K)]
                block = slice(r0, r0 + MOE_ROW_BLOCK)
                buf_ref[cur, block, 0:D_MODEL] = jnp.dot(own[0] + own[1], hn, preferred_element_type=F32)
                buf_ref[cur, block, D_MODEL:] = sum(jnp.dot(own[k], pieces[k], preferred_element_type=F32)
                                                    for k in range(TOP_K))

    _start_segments(copy_from(cur), off_ref, slot_ref, nch_ref, t)

    @pl.when(t == pl.num_programs(0) - 1)
    def _():
        @pl.when(t >= 1)
        def _():
            _wait_segments(copy_from(1 - cur), nch_ref, t - 1)

        _wait_segments(copy_from(cur), nch_ref, t)

        zero_ref[...] = jnp.zeros(zero_ref.shape, F32)

        def fill(unused_row, slot_row, rows):
            del unused_row
            return pltpu.make_async_copy(zero_ref.at[pl.ds(0, rows)], xs_ref.at[pl.ds(slot_row, rows)], sem.at[0])

        for e in range(N_EXPERTS):
            _start_pieces(fill, 0, tail_ref[e], ntail_ref[e])
        _wait_pieces(fill, [ntail_ref[e] for e in range(N_EXPERTS)])


def _dispatch(seg_tables, tail_tables, pos_t, hns, gates, n_slots):
    group_tiles = [0]
    for hn in hns:
        group_tiles.append(group_tiles[-1] + hn.shape[0] // MOE_TILE)

    def group_spec(g, cols):
        first, count = group_tiles[g], group_tiles[g + 1] - group_tiles[g]
        return pl.BlockSpec((MOE_TILE, cols), lambda i, *_: (jnp.clip(i - first, 0, count - 1), 0))

    groups = range(len(hns))
    grid_spec = pltpu.PrefetchScalarGridSpec(
        num_scalar_prefetch=5,
        grid=(group_tiles[-1],),
        in_specs=[pl.BlockSpec((SUBLANES, MOE_TILE), lambda i, *_: (0, i))]
                 + [group_spec(g, D_MODEL) for g in groups] + [group_spec(g, 128) for g in groups],
        out_specs=pl.BlockSpec(memory_space=pl.ANY),
        scratch_shapes=[pltpu.VMEM((2, MOE_ROWS, SLOT_W), F32), pltpu.VMEM((MOE_BIG, SLOT_W), F32),
                        pltpu.SemaphoreType.DMA((2,))],
    )
    return pl.pallas_call(
        functools.partial(_dispatch_kernel, group_tiles=tuple(group_tiles)),
        grid_spec=grid_spec,
        out_shape=jax.ShapeDtypeStruct((n_slots, SLOT_W), F32),
        compiler_params=_params(("arbitrary",)),
        name="moe_dispatch",
    )(*seg_tables, *tail_tables, pos_t, *hns, *gates)


def _expert_kernel(te_ref, tv_ref, x_ref, w1_ref, w3_ref, w2_ref, y_ref, act_ref):
    del te_ref
    valid = tv_ref[pl.program_id(0)] == 1
    n_chunks = act_ref.shape[0]

    @pl.when(valid)
    def _():
        xb = x_ref[:, 0:D_MODEL].astype(BF16)
        gate = jnp.sum(x_ref[:, D_MODEL:], axis=-1, keepdims=True)
        for c in range(n_chunks):
            cols = slice(c * EXPERT_FF_CHUNK, (c + 1) * EXPERT_FF_CHUNK)
            h1 = jnp.dot(xb, w1_ref[:, cols], preferred_element_type=F32)
            h3 = jnp.dot(xb, w3_ref[:, cols], preferred_element_type=F32)
            act_ref[c] = (h1 * jax.nn.sigmoid(h1) * h3).astype(BF16)
        acc = jnp.dot(act_ref[0], w2_ref[0:EXPERT_FF_CHUNK, :], preferred_element_type=F32)
        for c in range(1, n_chunks):
            acc = acc + jnp.dot(act_ref[c], w2_ref[c * EXPERT_FF_CHUNK:(c + 1) * EXPERT_FF_CHUNK, :],
                                preferred_element_type=F32)
        y_ref[...] = acc * gate

    @pl.when(jnp.logical_not(valid))
    def _():
        y_ref[...] = jnp.zeros(y_ref.shape, F32)


def _experts(tile_expert, tile_valid, xs, w1, w3, w2):
    n_slots = xs.shape[0]
    per_expert = dict(pipeline_mode=pl.Buffered(1))
    grid_spec = pltpu.PrefetchScalarGridSpec(
        num_scalar_prefetch=2,
        grid=(n_slots // EXPERT_TILE,),
        in_specs=[pl.BlockSpec((EXPERT_TILE, SLOT_W), lambda j, te, tv: (jnp.where(tv[j] == 1, j, 0), 0)),
                  pl.BlockSpec((None, D_MODEL, D_FF_EXPERT), lambda j, te, tv: (te[j], 0, 0), **per_expert),
                  pl.BlockSpec((None, D_MODEL, D_FF_EXPERT), lambda j, te, tv: (te[j], 0, 0), **per_expert),
                  pl.BlockSpec((None, D_FF_EXPERT, D_MODEL), lambda j, te, tv: (te[j], 0, 0), **per_expert)],
        out_specs=pl.BlockSpec((EXPERT_TILE, D_MODEL), lambda j, te, tv: (j, 0)),
        scratch_shapes=[pltpu.VMEM((D_FF_EXPERT // EXPERT_FF_CHUNK, EXPERT_TILE, EXPERT_FF_CHUNK), BF16)],
    )
    return pl.pallas_call(
        _expert_kernel,
        grid_spec=grid_spec,
        out_shape=jax.ShapeDtypeStruct((n_slots, D_MODEL), F32),
        compiler_params=_params(("arbitrary",)),
        name="moe_experts",
    )(tile_expert, tile_valid, xs, w1, w3, w2)


def _combine_kernel(off_ref, slot_ref, nch_ref, pos_ref, x_ref, y_ref, o_ref, buf_ref, sem):
    t = pl.program_id(0)
    cur = lax.rem(t, 2)

    def copy_into(b):
        def copy(buf_row, slot_row, rows):
            return pltpu.make_async_copy(y_ref.at[pl.ds(slot_row, rows)], buf_ref.at[b, pl.ds(buf_row, rows)],
                                         sem.at[b])
        return copy

    @pl.when(t == 0)
    def _():
        buf_ref[...] = jnp.zeros(buf_ref.shape, F32)
        _start_segments(copy_into(0), off_ref, slot_ref, nch_ref, 0)

    @pl.when(t + 1 < pl.num_programs(0))
    def _():
        _start_segments(copy_into(1 - cur), off_ref, slot_ref, nch_ref, t + 1)

    _wait_segments(copy_into(cur), nch_ref, t)
    yb = buf_ref[cur].astype(BF16)
    cols = lax.broadcasted_iota(jnp.int32, (MOE_TILE, MOE_ROWS), 1)
    pos = pos_ref[...]
    pick = jnp.where(cols == pos[:, 0:1], 1.0, jnp.where(cols == pos[:, 1:2], 1.0, 0.0)).astype(BF16)
    o_ref[...] = x_ref[...] + jnp.dot(pick, yb, preferred_element_type=F32)


def _combine(seg_tables, pos, x3, y):
    n = x3.shape[0]
    grid_spec = pltpu.PrefetchScalarGridSpec(
        num_scalar_prefetch=3,
        grid=(n // MOE_TILE,),
        in_specs=[pl.BlockSpec((MOE_TILE, 128), lambda i, *_: (i, 0)),
                  pl.BlockSpec((MOE_TILE, D_MODEL), lambda i, *_: (i, 0)),
                  pl.BlockSpec(memory_space=pl.ANY)],
        out_specs=pl.BlockSpec((MOE_TILE, D_MODEL), lambda i, *_: (i, 0)),
        scratch_shapes=[pltpu.VMEM((2, MOE_ROWS, D_MODEL), F32), pltpu.SemaphoreType.DMA((2,))],
    )
    return pl.pallas_call(
        _combine_kernel,
        grid_spec=grid_spec,
        out_shape=jax.ShapeDtypeStruct((n, D_MODEL), F32),
        compiler_params=_params(("arbitrary",)),
        name="moe_combine",
    )(*seg_tables, pos, x3, y)


def _route(counts):
    n_tok_tiles = counts.shape[0]
    seg = (counts + MOE_CHUNK - 1) // MOE_CHUNK * MOE_CHUNK
    seg_off = jnp.cumsum(seg, axis=1) - seg
    used = jnp.sum(seg, axis=0)
    padded = (used + EXPERT_TILE - 1) // EXPERT_TILE * EXPERT_TILE
    pad_end = jnp.cumsum(padded)
    pad_start = pad_end - padded
    seg_slot = pad_start[None, :] + jnp.cumsum(seg, axis=0) - seg
    worst_rows = n_tok_tiles * (TOP_K * MOE_TILE + N_EXPERTS * (MOE_CHUNK - 1)) + N_EXPERTS * (EXPERT_TILE - 1)
    n_tiles = (worst_rows + EXPERT_TILE - 1) // EXPERT_TILE
    tile_start = jnp.arange(n_tiles, dtype=jnp.int32) * EXPERT_TILE
    tile_expert = jnp.sum((tile_start[:, None] >= pad_end[None, :]).astype(jnp.int32), axis=1)
    tile_expert = jnp.minimum(tile_expert, N_EXPERTS - 1)
    tile_valid = (tile_start < pad_end[-1]).astype(jnp.int32)
    segs = (seg_off, seg_slot, seg // MOE_CHUNK)
    tail_end = pad_end.at[N_EXPERTS - 1].set(n_tiles * EXPERT_TILE)
    tails = (pad_start + used, (tail_end - pad_start - used) // MOE_CHUNK)
    return segs, tails, tile_expert, tile_valid, n_tiles


def _trunk(groups, meta_tokens, norm_mix_e, w_in, q_gain, k_gain, rel_bias, conv_w, conv_b,
           conv_ln_g, conv_ln_b, w_out, norm_ffn_e, ffn_w1, ffn_w3, ffn_w2,
           norm_mix_o, pool_w, pool_scale, norm_ffn_o, router_w, moe_w1, moe_w3, moe_w2):
    row = lambda a: a.reshape(1, -1).astype(F32)
    shapes = [(g.shape[0], g.shape[1]) for g in groups]
    xs = [g.reshape(-1, D_MODEL) for g in groups]

    hsum = jnp.asarray(np.kron(np.eye(N_HEADS), np.ones((HEAD_DIM, HEAD_DIM))), BF16)
    qg = row(jnp.tile(q_gain[0], N_HEADS))
    kg = row(jnp.tile(k_gain[0], N_HEADS))
    w_in_b = w_in[0].astype(BF16)
    g_mix = row(norm_mix_e[0])
    bias_pat = _attn_bias_patterns(rel_bias[0])
    conv_w_p = jnp.concatenate([conv_w[0].astype(F32), jnp.zeros((1, CONV_DIM), F32)], axis=0)
    w_out_a = w_out[0, :ATTN_DIM].astype(BF16)
    w_out_c = w_out[0, ATTN_DIM:].astype(BF16)
    g_ffn = row(norm_ffn_e[0])
    w1b, w3b, w2b = ffn_w1[0].astype(BF16), ffn_w3[0].astype(BF16), ffn_w2[0].astype(BF16)

    meta = meta_tokens.astype(F32)
    qm, km, vm, cm = _proj(meta, g_mix, w_in_b, qg, kg, hsum, N_META)
    attn_m = _meta_attention(qm, km, vm)

    x2s, x2ms = [], []
    for x, (bsz, seq) in zip(xs, shapes):
        q, k, v, c = _proj(x, g_mix, w_in_b, qg, kg, hsum, TOKEN_TILE)
        attn = _attention(q, k, v, km, vm, bias_pat, bsz, seq)
        cact, cact_m = _conv(c, cm, conv_w_p, row(conv_b[0]), row(conv_ln_g[0]), row(conv_ln_b[0]),
                             bsz, seq, TOKEN_TILE)
        x2s.append(_out_ffn(x, attn, cact, w_out_a, w_out_c, g_ffn, w1b, w3b, w2b, TOKEN_TILE))
        x2ms.append(_out_ffn(jnp.tile(meta, (bsz, 1)), jnp.tile(attn_m, (bsz, 1)), cact_m, w_out_a, w_out_c,
                             g_ffn, w1b, w3b, w2b, bsz * N_META))

    g_pool = row(norm_mix_o[0])
    w_pool_b = pool_w[0].astype(BF16)
    g_moe = row(norm_ffn_o[0])
    rw = jnp.zeros((D_MODEL, 128), F32).at[:, :N_EXPERTS].set(router_w[0].astype(F32))
    r_hi = rw.astype(BF16)
    r_lo = (rw - r_hi.astype(F32)).astype(BF16)
    mw1, mw3, mw2 = moe_w1[0].astype(BF16), moe_w3[0].astype(BF16), moe_w2[0].astype(BF16)

    assert TOKEN_TILE == MOE_TILE
    routed = [_pool_router(x2, x2m, g_pool, w_pool_b, row(pool_scale[0]), g_moe, r_hi, r_lo, bsz, seq, MOE_TILE)
              for x2, x2m, (bsz, seq) in zip(x2s, x2ms, shapes)]
    counts = jnp.concatenate([r[5][:, 0, :N_EXPERTS] for r in routed], axis=0)
    segs, tails, tile_expert, tile_valid, n_tiles = _route(counts)
    pos_t = jnp.concatenate([r[4] for r in routed], axis=1)
    slots = _dispatch(tuple(a.reshape(-1) for a in segs), tails, pos_t, [r[1] for r in routed],
                      [r[2] for r in routed], n_tiles * EXPERT_TILE)
    y = _experts(tile_expert, tile_valid, slots, mw1, mw3, mw2)
    outs, t0 = [], 0
    for (x3, _, _, pos, _, cnt), (bsz, seq) in zip(routed, shapes):
        t1 = t0 + cnt.shape[0]
        seg_tables = tuple(a[t0:t1].reshape(-1) for a in segs)
        outs.append(_combine(seg_tables, pos, x3, y).reshape(bsz, seq, D_MODEL))
        t0 = t1
    return tuple(outs)


def kernel(x_prompt, x_sample, meta_tokens, norm_mix_e, w_in, q_gain, k_gain, rel_bias, conv_w, conv_b, conv_ln_g, conv_ln_b, w_out, norm_ffn_e, ffn_w1, ffn_w3, ffn_w2, norm_mix_o, pool_w, pool_scale, norm_ffn_o, router_w, moe_w1, moe_w3, moe_w2):
    return _trunk([x_prompt, x_sample], meta_tokens, norm_mix_e, w_in, q_gain, k_gain, rel_bias, conv_w,
                  conv_b, conv_ln_g, conv_ln_b, w_out, norm_ffn_e, ffn_w1, ffn_w3, ffn_w2,
                  norm_mix_o, pool_w, pool_scale, norm_ffn_o, router_w, moe_w1, moe_w3, moe_w2)
```

```python
import functools

import numpy as np
import jax
import jax.numpy as jnp
from jax import lax
from jax.experimental import pallas as pl
from jax.experimental.pallas import tpu as pltpu

D_MODEL = 1024
N_META = 16
GRID_W = 64
N_HEADS = 8
HEAD_DIM = 64
ATTN_DIM = N_HEADS * HEAD_DIM
CONV_DIM = D_MODEL - ATTN_DIM
IN_DIM = 3 * ATTN_DIM + 2 * CONV_DIM
WIN_ROWS = 8
WIN_COLS = 16
CONV_WIDTH = 31
CONV_HALF = CONV_WIDTH // 2
POOL_WINDOWS = (2, 4, 8, 16)
POOL_GROUP_DIM = D_MODEL // len(POOL_WINDOWS)
D_FF = 2816
N_EXPERTS = 8
TOP_K = 2
D_FF_EXPERT = 3584
EPS = 1e-6

F32 = jnp.float32
BF16 = jnp.bfloat16
SUBLANES = 8

TOKEN_TILE = 512
ATTN_ROWS = 4
ATTN_Q = ATTN_ROWS * GRID_W
ATTN_KROWS = ATTN_ROWS + WIN_ROWS
ATTN_KBLK = ATTN_KROWS * GRID_W // ATTN_Q
ATTN_GROUP = 4
CONV_HALO = 16
CONV_CHUNK = 64
POOL_HALO = 8
EXPERT_TILE = 1024
EXPERT_FF_CHUNK = 512
MOE_TILE = 512
MOE_CHUNK = 8
MOE_BIG = 32
MOE_ROWS = 1152
MOE_ROW_BLOCK = 384
SLOT_W = D_MODEL + 128
NEG_INF = -1e30
VMEM_LIMIT = 56 * 1024 * 1024


def _params(sem):
    return pltpu.CompilerParams(dimension_semantics=sem, vmem_limit_bytes=VMEM_LIMIT)


def _const_spec(shape):
    zeros = (0,) * len(shape)
    return pl.BlockSpec(shape, lambda *_: zeros)


def _row_spec(tm, cols):
    return pl.BlockSpec((tm, cols), lambda i: (i, 0))


def _rms(x, gain):
    ms = jnp.mean(x * x, axis=-1, keepdims=True)
    return x * lax.rsqrt(ms + EPS) * gain


def _proj_kernel(x_ref, g_ref, w_ref, qg_ref, kg_ref, hsum_ref, q_ref, k_ref, v_ref, c_ref):
    n = _rms(x_ref[...], g_ref[...]).astype(BF16)
    proj = jnp.dot(n, w_ref[...], preferred_element_type=F32)
    q = proj[:, :ATTN_DIM]
    k = proj[:, ATTN_DIM:2 * ATTN_DIM]
    v = proj[:, 2 * ATTN_DIM:3 * ATTN_DIM]
    u = proj[:, 3 * ATTN_DIM:3 * ATTN_DIM + CONV_DIM]
    g = proj[:, 3 * ATTN_DIM + CONV_DIM:]

    def head_norm(a, gain):
        ss = jnp.dot((a * a).astype(BF16), hsum_ref[...], preferred_element_type=F32)
        return a * lax.rsqrt(ss * (1.0 / HEAD_DIM) + EPS) * gain

    q_ref[...] = (head_norm(q, qg_ref[...]) * (HEAD_DIM ** -0.5)).astype(BF16)
    k_ref[...] = head_norm(k, kg_ref[...]).astype(BF16)
    v_ref[...] = v.astype(BF16)
    c_ref[...] = (u * jax.nn.sigmoid(g)).astype(BF16)


def _proj(x, norm_g, w_in, qg, kg, hsum, tm):
    n = x.shape[0]
    out = jax.ShapeDtypeStruct((n, ATTN_DIM), BF16)
    return pl.pallas_call(
        _proj_kernel,
        grid=(n // tm,),
        in_specs=[_row_spec(tm, D_MODEL), _const_spec((1, D_MODEL)), _const_spec((D_MODEL, IN_DIM)),
                  _const_spec((1, ATTN_DIM)), _const_spec((1, ATTN_DIM)), _const_spec((ATTN_DIM, ATTN_DIM))],
        out_specs=[_row_spec(tm, ATTN_DIM)] * 4,
        out_shape=[out] * 4,
        compiler_params=_params(("arbitrary",)),
        name="in_proj",
    )(x, norm_g, w_in, qg, kg, hsum)


def _attn_bias_patterns(rel_bias):
    n_dr, n_dc = 2 * WIN_ROWS - 1, 2 * WIN_COLS - 1
    qr = np.arange(ATTN_ROWS)[:, None]
    kr = np.arange(ATTN_KROWS)[None, :]
    qc = np.arange(GRID_W)[:, None]
    kc = np.arange(GRID_W)[None, :]
    cs = np.clip(qc - WIN_COLS // 2, 0, GRID_W - WIN_COLS)
    col_ok = (kc >= cs) & (kc < cs + WIN_COLS)
    dc = np.clip(kc - qc + (WIN_COLS - 1), 0, n_dc - 1)
    col_sel = (dc[None] == np.arange(n_dc)[:, None, None]).astype(np.float32)
    row_sel, ok = [], []
    for off, rs in ((0, 0 * qr), (ATTN_ROWS, qr), (2 * ATTN_ROWS, ATTN_ROWS + 0 * qr)):
        row_ok = (kr >= rs) & (kr < rs + WIN_ROWS)
        dr = np.clip(kr - (off + qr) + (WIN_ROWS - 1), 0, n_dr - 1)
        row_sel.append((dr[..., None] == np.arange(n_dr)).astype(np.float32))
        ok.append(row_ok[:, None, :, None] & col_ok[None, :, None, :])
    row_sel = np.stack(row_sel)
    ok = np.stack(ok).reshape(3, 1, ATTN_Q, ATTN_KROWS * GRID_W)
    vals = jnp.einsum('pqka,hab,bcd->phqckd', row_sel, rel_bias.astype(F32), col_sel,
                      precision=lax.Precision.HIGHEST)
    vals = vals.reshape(3, N_HEADS, ATTN_Q, ATTN_KROWS * GRID_W)
    return jnp.where(ok, vals, NEG_INF)


def _attn_kernel(q_ref, k0_ref, k1_ref, k2_ref, v0_ref, v1_ref, v2_ref, km_ref, vm_ref, bias_ref, o_ref):
    nt = (((1,), (1,)), ((), ()))
    k_refs = (k0_ref, k1_ref, k2_ref)
    v_refs = (v0_ref, v1_ref, v2_ref)
    outs = []
    for g in range(N_HEADS // ATTN_GROUP):
        group_cols = slice(g * ATTN_GROUP * HEAD_DIM, (g + 1) * ATTN_GROUP * HEAD_DIM)
        p_loc = [[] for _ in range(ATTN_KBLK)]
        p_meta, denom = [], []
        for h in range(g * ATTN_GROUP, (g + 1) * ATTN_GROUP):
            sl = slice(h * HEAD_DIM, (h + 1) * HEAD_DIM)
            qh = q_ref[:, sl]
            s_meta = lax.dot_general(qh, km_ref[:, sl], nt, preferred_element_type=F32)
            s_loc = [lax.dot_general(qh, k_refs[t][:, sl], nt, preferred_element_type=F32)
                     + bias_ref[0, h, :, t * ATTN_Q:(t + 1) * ATTN_Q] for t in range(ATTN_KBLK)]
            m_loc = functools.reduce(jnp.maximum, s_loc)
            m = jnp.maximum(jnp.max(s_meta, axis=-1, keepdims=True), jnp.max(m_loc, axis=-1, keepdims=True))
            pm = jnp.exp(s_meta - m)
            p_sum = None
            for t in range(ATTN_KBLK):
                p = jnp.exp(s_loc[t] - m)
                p_sum = p if p_sum is None else p_sum + p
                p_loc[t].append(p.astype(BF16))
            p_meta.append(pm.astype(BF16))
            denom.append(jnp.sum(pm, axis=-1, keepdims=True) + jnp.sum(p_sum, axis=-1, keepdims=True))
        acc = jnp.dot(jnp.concatenate(p_meta, axis=0), vm_ref[:, group_cols], preferred_element_type=F32)
        for t in range(ATTN_KBLK):
            acc = acc + jnp.dot(jnp.concatenate(p_loc[t], axis=0), v_refs[t][:, group_cols],
                                preferred_element_type=F32)
        for i in range(ATTN_GROUP):
            rows = slice(i * ATTN_Q, (i + 1) * ATTN_Q)
            outs.append(acc[rows, i * HEAD_DIM:(i + 1) * HEAD_DIM] / denom[i])
    o_ref[...] = jnp.concatenate(outs, axis=-1).astype(BF16)


def _attention(q, k, v, km, vm, bias_pat, bsz, seq):
    nb = seq // ATTN_Q
    assert seq % ATTN_Q == 0 and nb >= ATTN_KBLK

    def q_map(b, j):
        return (b * nb + j, 0)

    def kv_map(t):
        return lambda b, j: (b * nb + jnp.clip(j - 1, 0, nb - ATTN_KBLK) + t, 0)

    def bias_map(b, j):
        return (jnp.where(j == 0, 0, jnp.where(j == nb - 1, 2, 1)), 0, 0, 0)

    blk = (ATTN_Q, ATTN_DIM)
    kv_specs = [pl.BlockSpec(blk, kv_map(t)) for t in range(ATTN_KBLK)]
    return pl.pallas_call(
        _attn_kernel,
        grid=(bsz, nb),
        in_specs=[pl.BlockSpec(blk, q_map)] + kv_specs + kv_specs
                 + [_const_spec((N_META, ATTN_DIM)), _const_spec((N_META, ATTN_DIM)),
                    pl.BlockSpec((1, N_HEADS, ATTN_Q, ATTN_KBLK * ATTN_Q), bias_map)],
        out_specs=pl.BlockSpec(blk, q_map),
        out_shape=jax.ShapeDtypeStruct((bsz * seq, ATTN_DIM), BF16),
        compiler_params=_params(("arbitrary", "arbitrary")),
        name="nbr_attn",
    )(q, k, k, k, v, v, v, km, vm, bias_pat)


def _meta_attn_kernel(q_ref, k_ref, v_ref, o_ref):
    nt = (((1,), (1,)), ((), ()))
    outs = []
    for h in range(N_HEADS):
        sl = slice(h * HEAD_DIM, (h + 1) * HEAD_DIM)
        s = lax.dot_general(q_ref[:, sl], k_ref[:, sl], nt, preferred_element_type=F32)
        p = jnp.exp(s - jnp.max(s, axis=-1, keepdims=True))
        o = jnp.dot(p.astype(BF16), v_ref[:, sl], preferred_element_type=F32)
        outs.append(o / jnp.sum(p, axis=-1, keepdims=True))
    o_ref[...] = jnp.concatenate(outs, axis=-1).astype(BF16)


def _meta_attention(qm, km, vm):
    spec = _const_spec((N_META, ATTN_DIM))
    return pl.pallas_call(
        _meta_attn_kernel,
        in_specs=[spec] * 3, out_specs=spec, grid=(1,),
        out_shape=jax.ShapeDtypeStruct((N_META, ATTN_DIM), BF16),
        compiler_params=_params(("arbitrary",)),
        name="meta_attn",
    )(qm, km, vm)


def _conv_kernel(c_ref, prev_ref, next_ref, cm_ref, w_ref, b_ref, lg_ref, lb_ref,
                 out_ref, outm_ref, xs_ref, mpad_ref):
    tile = c_ref.shape[0]
    n_pad = tile + 2 * CONV_HALO
    i = pl.program_id(1)
    first = i == 0
    last = i == pl.num_programs(1) - 1
    cm = cm_ref[...].astype(F32)
    xs_ref[0, 0:CONV_HALO, :] = jnp.where(first, cm, prev_ref[...].astype(F32))
    xs_ref[0, CONV_HALO:CONV_HALO + tile, :] = c_ref[...].astype(F32)
    xs_ref[0, CONV_HALO + tile:, :] = jnp.where(last, 0.0, next_ref[...].astype(F32))
    for s in range(1, SUBLANES):
        xs_ref[s, 0:n_pad - SUBLANES, :] = xs_ref[0, s:s + n_pad - SUBLANES, :]

    def conv_rows(read, rows):
        acc = jnp.zeros((rows, CONV_DIM), F32)
        for j in range(CONV_WIDTH):
            acc = acc + w_ref[j:j + 1, :] * read(j - CONV_HALF)
        y = acc + b_ref[...]
        mu = jnp.mean(y, axis=-1, keepdims=True)
        yc = y - mu
        var = jnp.mean(yc * yc, axis=-1, keepdims=True)
        z = yc * lax.rsqrt(var + EPS) * lg_ref[...] + lb_ref[...]
        return (z * jax.nn.sigmoid(z)).astype(BF16)

    for rc in range(tile // CONV_CHUNK):
        start = CONV_HALO + rc * CONV_CHUNK

        def read(d, start=start):
            lo = start + d
            return xs_ref[lo % SUBLANES, lo - lo % SUBLANES:lo - lo % SUBLANES + CONV_CHUNK, :]

        out_ref[rc * CONV_CHUNK:(rc + 1) * CONV_CHUNK, :] = conv_rows(read, CONV_CHUNK)

    @pl.when(first)
    def _():
        mpad_ref[0:N_META, :] = jnp.zeros((N_META, CONV_DIM), F32)
        mpad_ref[N_META:2 * N_META, :] = cm
        mpad_ref[2 * N_META:, :] = c_ref[0:N_META, :].astype(F32)
        outm_ref[...] = conv_rows(lambda d: mpad_ref[N_META + d:2 * N_META + d, :], N_META)


def _conv(c, cm, conv_w, conv_b, ln_g, ln_b, bsz, seq, tile):
    tps = seq // tile
    hpt = tile // CONV_HALO
    n_halo = bsz * seq // CONV_HALO

    def main_map(b, i):
        return (b * tps + i, 0)

    def prev_map(b, i):
        return (jnp.maximum((b * tps + i) * hpt - 1, 0), 0)

    def next_map(b, i):
        return (jnp.minimum((b * tps + i + 1) * hpt, n_halo - 1), 0)

    halo = (CONV_HALO, CONV_DIM)
    vec = _const_spec((1, CONV_DIM))
    return pl.pallas_call(
        _conv_kernel,
        grid=(bsz, tps),
        in_specs=[pl.BlockSpec((tile, CONV_DIM), main_map), pl.BlockSpec(halo, prev_map),
                  pl.BlockSpec(halo, next_map), _const_spec((N_META, CONV_DIM)),
                  _const_spec((CONV_WIDTH + 1, CONV_DIM)), vec, vec, vec],
        out_specs=[pl.BlockSpec((tile, CONV_DIM), main_map),
                   pl.BlockSpec((N_META, CONV_DIM), lambda b, i: (b, 0))],
        out_shape=[jax.ShapeDtypeStruct((bsz * seq, CONV_DIM), BF16),
                   jax.ShapeDtypeStruct((bsz * N_META, CONV_DIM), BF16)],
        scratch_shapes=[pltpu.VMEM((SUBLANES, tile + 2 * CONV_HALO, CONV_DIM), F32),
                        pltpu.VMEM((3 * N_META, CONV_DIM), F32)],
        compiler_params=_params(("arbitrary", "arbitrary")),
        name="conv_ln_silu",
    )(c, c, c, cm, conv_w, conv_b, ln_g, ln_b)


def _out_ffn_kernel(x_ref, a_ref, c_ref, wa_ref, wc_ref, g_ref, w1_ref, w3_ref, w2_ref, o_ref):
    x = (x_ref[...]
         + jnp.dot(a_ref[...], wa_ref[...], preferred_element_type=F32)
         + jnp.dot(c_ref[...], wc_ref[...], preferred_element_type=F32))
    n = _rms(x, g_ref[...]).astype(BF16)
    h1 = jnp.dot(n, w1_ref[...], preferred_element_type=F32)
    h3 = jnp.dot(n, w3_ref[...], preferred_element_type=F32)
    act = (h1 * jax.nn.sigmoid(h1) * h3).astype(BF16)
    o_ref[...] = x + jnp.dot(act, w2_ref[...], preferred_element_type=F32)


def _out_ffn(x, attn, cact, w_attn, w_conv, g, w1, w3, w2, tm):
    n = x.shape[0]
    wspec = _const_spec((ATTN_DIM, D_MODEL))
    return pl.pallas_call(
        _out_ffn_kernel,
        grid=(n // tm,),
        in_specs=[_row_spec(tm, D_MODEL), _row_spec(tm, ATTN_DIM), _row_spec(tm, CONV_DIM), wspec, wspec,
                  _const_spec((1, D_MODEL)), _const_spec((D_MODEL, D_FF)), _const_spec((D_MODEL, D_FF)),
                  _const_spec((D_FF, D_MODEL))],
        out_specs=_row_spec(tm, D_MODEL),
        out_shape=jax.ShapeDtypeStruct((n, D_MODEL), F32),
        compiler_params=_params(("arbitrary",)),
        name="out_proj_swiglu",
    )(x, attn, cact, w_attn, w_conv, g, w1, w3, w2)


def _pool_router_kernel(x_ref, prev_ref, next_ref, xm_ref, g_ref, wp_ref, ps_ref, g2_ref, rhi_ref, rlo_ref,
                        x3_ref, hn_ref, gate_ref, pos_ref, pos_t_ref, cnt_ref, npad_ref, *sum_refs, seq):
    tile = x_ref.shape[0]
    i = pl.program_id(1)
    first = i == 0
    last = i == pl.num_programs(1) - 1
    gain = g_ref[...]
    x = x_ref[...]
    n_main = _rms(x, gain)
    npad_ref[0:POOL_HALO, :] = _rms(jnp.where(first, xm_ref[...], prev_ref[...]), gain)
    npad_ref[POOL_HALO:POOL_HALO + tile, :] = n_main
    npad_ref[POOL_HALO + tile:, :] = jnp.where(last, 0.0, _rms(next_ref[...], gain))

    def window_sum(cols, w):
        first = POOL_HALO - w // 2
        needs, need, m = [], first + tile, w // 2
        while m >= 1:
            need += m
            needs.append((m, need))
            m //= 2
        needs.reverse()
        read = lambda lo, n: npad_ref[lo:lo + n, cols]
        for level, ((m, _), (_, n_next)) in enumerate(zip(needs, needs[1:])):
            buf = sum_refs[level % 2]
            buf[0:n_next, :] = read(0, n_next) + read(m, n_next)
            read = lambda lo, n, buf=buf: buf[lo:lo + n, :]
        return read(first, tile) + read(first + w // 2, tile)

    tok = i * tile + lax.broadcasted_iota(jnp.int32, (tile, 1), 0)
    mixed = []
    for gi, w in enumerate(POOL_WINDOWS):
        cols = slice(gi * POOL_GROUP_DIM, (gi + 1) * POOL_GROUP_DIM)
        half = w // 2
        acc = window_sum(cols, w)
        count = (w - jnp.maximum(tok + half - seq, 0)).astype(F32)
        diff = (acc / count - n_main[:, cols]).astype(BF16)
        mixed.append(jnp.dot(diff, wp_ref[gi], preferred_element_type=F32))
    x3 = x + jnp.concatenate(mixed, axis=-1) * ps_ref[...]
    x3_ref[...] = x3

    hn = _rms(x3, g2_ref[...])
    hn_ref[...] = hn
    h_hi = hn.astype(BF16)
    h_lo = (hn - h_hi.astype(F32)).astype(BF16)
    def logits_of(rows):
        return (jnp.dot(h_hi[rows], rhi_ref[...], preferred_element_type=F32)
                + jnp.dot(h_lo[rows], rhi_ref[...], preferred_element_type=F32)
                + jnp.dot(h_hi[rows], rlo_ref[...], preferred_element_type=F32))

    logits = jnp.concatenate([logits_of(slice(0, tile // 2)), logits_of(slice(tile // 2, tile))],
                             axis=0)
    lane = lax.broadcasted_iota(jnp.int32, logits.shape, 1)
    lane_f = lane.astype(F32)
    logits = jnp.where(lane < N_EXPERTS, logits, NEG_INF)
    m1 = jnp.max(logits, axis=-1, keepdims=True)
    i1 = jnp.min(jnp.where(logits == m1, lane_f, 256.0), axis=-1, keepdims=True)
    rest = jnp.where(lane_f == i1, NEG_INF, logits)
    m2 = jnp.max(rest, axis=-1, keepdims=True)
    i2 = jnp.min(jnp.where(rest == m2, lane_f, 256.0), axis=-1, keepdims=True)
    e2 = jnp.exp(m2 - m1)
    g1 = 1.0 / (1.0 + e2)
    gate_ref[...] = jnp.where(lane == 0, g1, jnp.where(lane == 1, e2 * g1, 0.0))

    hot = [jnp.where(lane_f == i1, 1.0, 0.0), jnp.where(lane_f == i2, 1.0, 0.0)]
    earlier = jnp.where(lax.broadcasted_iota(jnp.int32, (tile, tile), 1)
                        < lax.broadcasted_iota(jnp.int32, (tile, tile), 0), 1.0, 0.0).astype(BF16)
    before = [jnp.dot(earlier, h.astype(BF16), preferred_element_type=F32) for h in hot]
    count = [jnp.sum(h, axis=0, keepdims=True) for h in hot]
    total = count[0] + count[1]
    chunks = jnp.floor((total + (MOE_CHUNK - 1)) * (1.0 / MOE_CHUNK))
    lower_expert = jnp.where(lax.broadcasted_iota(jnp.int32, (128, 128), 0)
                             < lax.broadcasted_iota(jnp.int32, (128, 128), 1), 1.0, 0.0).astype(BF16)
    seg_off = MOE_CHUNK * jnp.dot(jnp.broadcast_to(chunks, (SUBLANES, 128)).astype(BF16), lower_expert,
                                  preferred_element_type=F32)[0:1, :]
    pos0 = jnp.sum(hot[0] * (before[0] + seg_off), axis=-1, keepdims=True)
    pos1 = jnp.sum(hot[1] * (before[1] + count[0] + seg_off), axis=-1, keepdims=True)
    pos = jnp.where(lane == 0, pos0, jnp.where(lane == 1, pos1, 0.0))
    pos_ref[...] = pos.astype(jnp.int32)
    pos_t_ref[...] = jnp.transpose(pos)[0:SUBLANES, :].astype(jnp.int32)
    cnt_ref[...] = jnp.broadcast_to(total, (SUBLANES, 128)).astype(jnp.int32)


def _pool_router(x2, x2m, g, w_pool, pool_scale, g2, r_hi, r_lo, bsz, seq, tile):
    tps = seq // tile
    hpt = tile // POOL_HALO
    n_halo = bsz * seq // POOL_HALO

    def main_map(b, i):
        return (b * tps + i, 0)

    def prev_map(b, i):
        return (jnp.maximum((b * tps + i) * hpt - 1, 0), 0)

    def next_map(b, i):
        return (jnp.minimum((b * tps + i + 1) * hpt, n_halo - 1), 0)

    halo = (POOL_HALO, D_MODEL)
    vec = _const_spec((1, D_MODEL))
    main = pl.BlockSpec((tile, D_MODEL), main_map)
    small = pl.BlockSpec((tile, 128), main_map)
    n = bsz * seq
    return pl.pallas_call(
        functools.partial(_pool_router_kernel, seq=seq),
        grid=(bsz, tps),
        in_specs=[main, pl.BlockSpec(halo, prev_map), pl.BlockSpec(halo, next_map),
                  pl.BlockSpec(halo, lambda b, i: (2 * b + 1, 0)), vec,
                  _const_spec((len(POOL_WINDOWS), POOL_GROUP_DIM, POOL_GROUP_DIM)), vec, vec,
                  _const_spec((D_MODEL, 128)), _const_spec((D_MODEL, 128))],
        out_specs=[main, main, small, small,
                   pl.BlockSpec((SUBLANES, tile), lambda b, i: (0, b * tps + i)),
                   pl.BlockSpec((None, SUBLANES, 128), lambda b, i: (b * tps + i, 0, 0))],
        out_shape=[jax.ShapeDtypeStruct((n, D_MODEL), F32), jax.ShapeDtypeStruct((n, D_MODEL), F32),
                   jax.ShapeDtypeStruct((n, 128), F32), jax.ShapeDtypeStruct((n, 128), jnp.int32),
                   jax.ShapeDtypeStruct((SUBLANES, n), jnp.int32),
                   jax.ShapeDtypeStruct((n // tile, SUBLANES, 128), jnp.int32)],
        scratch_shapes=[pltpu.VMEM((tile + 2 * POOL_HALO, D_MODEL), F32)]
                       + [pltpu.VMEM((tile + 2 * POOL_HALO, POOL_GROUP_DIM), F32)] * 2,
        compiler_params=_params(("arbitrary", "arbitrary")),
        name="pool_router",
    )(x2, x2, x2, x2m, g, w_pool, pool_scale, g2, r_hi, r_lo)


def _split_chunks(n_chunks):
    per_big = MOE_BIG // MOE_CHUNK
    n_big = lax.div(n_chunks, per_big)
    return n_big, n_chunks - n_big * per_big


def _start_pieces(copy, first_a, first_b, n_chunks):
    n_big, n_small = _split_chunks(n_chunks)

    def start(rows, base_a, base_b):
        def body(c, carry):
            copy(pl.multiple_of(base_a + c * rows, MOE_CHUNK), pl.multiple_of(base_b + c * rows, MOE_CHUNK),
                 rows).start()
            return carry
        return body

    lax.fori_loop(0, n_big, start(MOE_BIG, first_a, first_b), 0)
    lax.fori_loop(0, n_small, start(MOE_CHUNK, first_a + n_big * MOE_BIG, first_b + n_big * MOE_BIG), 0)


def _wait_pieces(copy, chunk_counts):
    splits = [_split_chunks(n) for n in chunk_counts]
    for rows, count in ((MOE_BIG, sum(s[0] for s in splits)), (MOE_CHUNK, sum(s[1] for s in splits))):
        def wait(c, carry, rows=rows):
            copy(0, 0, rows).wait()
            return carry

        lax.fori_loop(0, count, wait, 0)


def _start_segments(copy, off_ref, slot_ref, nch_ref, t):
    for e in range(N_EXPERTS):
        _start_pieces(copy, off_ref[t * N_EXPERTS + e], slot_ref[t * N_EXPERTS + e], nch_ref[t * N_EXPERTS + e])


def _wait_segments(copy, nch_ref, t):
    _wait_pieces(copy, [nch_ref[t * N_EXPERTS + e] for e in range(N_EXPERTS)])


def _dispatch_kernel(off_ref, slot_ref, nch_ref, tail_ref, ntail_ref, pos_ref, *refs, group_tiles):
    n_groups = len(group_tiles) - 1
    hn_refs, gate_refs = refs[:n_groups], refs[n_groups:2 * n_groups]
    xs_ref, buf_ref, zero_ref, sem = refs[2 * n_groups:]
    t = pl.program_id(0)
    cur = lax.rem(t, 2)

    def copy_from(b):
        def copy(buf_row, slot_row, rows):
            return pltpu.make_async_copy(buf_ref.at[b, pl.ds(buf_row, rows)], xs_ref.at[pl.ds(slot_row, rows)],
                                         sem.at[b])
        return copy

    @pl.when(t >= 2)
    def _():
        _wait_segments(copy_from(cur), nch_ref, t - 2)

    for g in range(n_groups):
        @pl.when((t >= group_tiles[g]) & (t < group_tiles[g + 1]))
        def _(g=g):
            hn = hn_refs[g][...].astype(BF16)
            gates = gate_refs[g][...]
            lane = lax.broadcasted_iota(jnp.int32, gates.shape, 1)
            pieces = []
            for k in range(TOP_K):
                gk = gates[:, k:k + 1]
                hi = gk.astype(BF16).astype(F32)
                mid = (gk - hi).astype(BF16).astype(F32)
                pieces.append(jnp.where(lane == 0, hi, jnp.where(lane == 1, mid, jnp.where(lane == 2, gk - hi - mid, 0.0)))
                              .astype(BF16))
            for r0 in range(0, MOE_ROWS, MOE_ROW_BLOCK):
                rows = r0 + lax.broadcasted_iota(jnp.int32, (MOE_ROW_BLOCK, MOE_TILE), 0)
                own = [jnp.where(rows == pos_ref[k:k + 1, :], 1.0, 0.0).astype(BF16) for k in range(TOP_K)]
                block = slice(r0, r0 + MOE_ROW_BLOCK)
                buf_ref[cur, block, 0:D_MODEL] = jnp.dot(own[0] + own[1], hn, preferred_element_type=F32)
                buf_ref[cur, block, D_MODEL:] = sum(jnp.dot(own[k], pieces[k], preferred_element_type=F32)
                                                    for k in range(TOP_K))

    _start_segments(copy_from(cur), off_ref, slot_ref, nch_ref, t)

    @pl.when(t == pl.num_programs(0) - 1)
    def _():
        @pl.when(t >= 1)
        def _():
            _wait_segments(copy_from(1 - cur), nch_ref, t - 1)

        _wait_segments(copy_from(cur), nch_ref, t)

        zero_ref[...] = jnp.zeros(zero_ref.shape, F32)

        def fill(unused_row, slot_row, rows):
            del unused_row
            return pltpu.make_async_copy(zero_ref.at[pl.ds(0, rows)], xs_ref.at[pl.ds(slot_row, rows)], sem.at[0])

        for e in range(N_EXPERTS):
            _start_pieces(fill, 0, tail_ref[e], ntail_ref[e])
        _wait_pieces(fill, [ntail_ref[e] for e in range(N_EXPERTS)])


def _dispatch(seg_tables, tail_tables, pos_t, hns, gates, n_slots):
    group_tiles = [0]
    for hn in hns:
        group_tiles.append(group_tiles[-1] + hn.shape[0] // MOE_TILE)

    def group_spec(g, cols):
        first, count = group_tiles[g], group_tiles[g + 1] - group_tiles[g]
        return pl.BlockSpec((MOE_TILE, cols), lambda i, *_: (jnp.clip(i - first, 0, count - 1), 0))

    groups = range(len(hns))
    grid_spec = pltpu.PrefetchScalarGridSpec(
        num_scalar_prefetch=5,
        grid=(group_tiles[-1],),
        in_specs=[pl.BlockSpec((SUBLANES, MOE_TILE), lambda i, *_: (0, i))]
                 + [group_spec(g, D_MODEL) for g in groups] + [group_spec(g, 128) for g in groups],
        out_specs=pl.BlockSpec(memory_space=pl.ANY),
        scratch_shapes=[pltpu.VMEM((2, MOE_ROWS, SLOT_W), F32), pltpu.VMEM((MOE_BIG, SLOT_W), F32),
                        pltpu.SemaphoreType.DMA((2,))],
    )
    return pl.pallas_call(
        functools.partial(_dispatch_kernel, group_tiles=tuple(group_tiles)),
        grid_spec=grid_spec,
        out_shape=jax.ShapeDtypeStruct((n_slots, SLOT_W), F32),
        compiler_params=_params(("arbitrary",)),
        name="moe_dispatch",
    )(*seg_tables, *tail_tables, pos_t, *hns, *gates)


def _expert_kernel(te_ref, tv_ref, x_ref, w1_ref, w3_ref, w2_ref, y_ref, act_ref):
    del te_ref
    valid = tv_ref[pl.program_id(0)] == 1
    n_chunks = act_ref.shape[0]

    @pl.when(valid)
    def _():
        xb = x_ref[:, 0:D_MODEL].astype(BF16)
        gate = jnp.sum(x_ref[:, D_MODEL:], axis=-1, keepdims=True)
        for c in range(n_chunks):
            cols = slice(c * EXPERT_FF_CHUNK, (c + 1) * EXPERT_FF_CHUNK)
            h1 = jnp.dot(xb, w1_ref[:, cols], preferred_element_type=F32)
            h3 = jnp.dot(xb, w3_ref[:, cols], preferred_element_type=F32)
            act_ref[c] = (h1 * jax.nn.sigmoid(h1) * h3).astype(BF16)
        acc = jnp.dot(act_ref[0], w2_ref[0:EXPERT_FF_CHUNK, :], preferred_element_type=F32)
        for c in range(1, n_chunks):
            acc = acc + jnp.dot(act_ref[c], w2_ref[c * EXPERT_FF_CHUNK:(c + 1) * EXPERT_FF_CHUNK, :],
                                preferred_element_type=F32)
        y_ref[...] = acc * gate

    @pl.when(jnp.logical_not(valid))
    def _():
        y_ref[...] = jnp.zeros(y_ref.shape, F32)


def _experts(tile_expert, tile_valid, xs, w1, w3, w2):
    n_slots = xs.shape[0]
    per_expert = dict(pipeline_mode=pl.Buffered(1))
    grid_spec = pltpu.PrefetchScalarGridSpec(
        num_scalar_prefetch=2,
        grid=(n_slots // EXPERT_TILE,),
        in_specs=[pl.BlockSpec((EXPERT_TILE, SLOT_W), lambda j, te, tv: (jnp.where(tv[j] == 1, j, 0), 0)),
                  pl.BlockSpec((None, D_MODEL, D_FF_EXPERT), lambda j, te, tv: (te[j], 0, 0), **per_expert),
                  pl.BlockSpec((None, D_MODEL, D_FF_EXPERT), lambda j, te, tv: (te[j], 0, 0), **per_expert),
                  pl.BlockSpec((None, D_FF_EXPERT, D_MODEL), lambda j, te, tv: (te[j], 0, 0), **per_expert)],
        out_specs=pl.BlockSpec((EXPERT_TILE, D_MODEL), lambda j, te, tv: (j, 0)),
        scratch_shapes=[pltpu.VMEM((D_FF_EXPERT // EXPERT_FF_CHUNK, EXPERT_TILE, EXPERT_FF_CHUNK), BF16)],
    )
    return pl.pallas_call(
        _expert_kernel,
        grid_spec=grid_spec,
        out_shape=jax.ShapeDtypeStruct((n_slots, D_MODEL), F32),
        compiler_params=_params(("arbitrary",)),
        name="moe_experts",
    )(tile_expert, tile_valid, xs, w1, w3, w2)


def _combine_kernel(off_ref, slot_ref, nch_ref, pos_ref, x_ref, y_ref, o_ref, buf_ref, sem):
    t = pl.program_id(0)
    cur = lax.rem(t, 2)

    def copy_into(b):
        def copy(buf_row, slot_row, rows):
            return pltpu.make_async_copy(y_ref.at[pl.ds(slot_row, rows)], buf_ref.at[b, pl.ds(buf_row, rows)],
                                         sem.at[b])
        return copy

    @pl.when(t == 0)
    def _():
        buf_ref[...] = jnp.zeros(buf_ref.shape, F32)
        _start_segments(copy_into(0), off_ref, slot_ref, nch_ref, 0)

    @pl.when(t + 1 < pl.num_programs(0))
    def _():
        _start_segments(copy_into(1 - cur), off_ref, slot_ref, nch_ref, t + 1)

    _wait_segments(copy_into(cur), nch_ref, t)
    yb = buf_ref[cur].astype(BF16)
    cols = lax.broadcasted_iota(jnp.int32, (MOE_TILE, MOE_ROWS), 1)
    pos = pos_ref[...]
    pick = jnp.where(cols == pos[:, 0:1], 1.0, jnp.where(cols == pos[:, 1:2], 1.0, 0.0)).astype(BF16)
    o_ref[...] = x_ref[...] + jnp.dot(pick, yb, preferred_element_type=F32)


def _combine(seg_tables, pos, x3, y):
    n = x3.shape[0]
    grid_spec = pltpu.PrefetchScalarGridSpec(
        num_scalar_prefetch=3,
        grid=(n // MOE_TILE,),
        in_specs=[pl.BlockSpec((MOE_TILE, 128), lambda i, *_: (i, 0)),
                  pl.BlockSpec((MOE_TILE, D_MODEL), lambda i, *_: (i, 0)),
                  pl.BlockSpec(memory_space=pl.ANY)],
        out_specs=pl.BlockSpec((MOE_TILE, D_MODEL), lambda i, *_: (i, 0)),
        scratch_shapes=[pltpu.VMEM((2, MOE_ROWS, D_MODEL), F32), pltpu.SemaphoreType.DMA((2,))],
    )
    return pl.pallas_call(
        _combine_kernel,
        grid_spec=grid_spec,
        out_shape=jax.ShapeDtypeStruct((n, D_MODEL), F32),
        compiler_params=_params(("arbitrary",)),
        name="moe_combine",
    )(*seg_tables, pos, x3, y)


def _route(counts):
    n_tok_tiles = counts.shape[0]
    seg = (counts + MOE_CHUNK - 1) // MOE_CHUNK * MOE_CHUNK
    seg_off = jnp.cumsum(seg, axis=1) - seg
    used = jnp.sum(seg, axis=0)
    padded = (used + EXPERT_TILE - 1) // EXPERT_TILE * EXPERT_TILE
    pad_end = jnp.cumsum(padded)
    pad_start = pad_end - padded
    seg_slot = pad_start[None, :] + jnp.cumsum(seg, axis=0) - seg
    worst_rows = n_tok_tiles * (TOP_K * MOE_TILE + N_EXPERTS * (MOE_CHUNK - 1)) + N_EXPERTS * (EXPERT_TILE - 1)
    n_tiles = (worst_rows + EXPERT_TILE - 1) // EXPERT_TILE
    tile_start = jnp.arange(n_tiles, dtype=jnp.int32) * EXPERT_TILE
    tile_expert = jnp.sum((tile_start[:, None] >= pad_end[None, :]).astype(jnp.int32), axis=1)
    tile_expert = jnp.minimum(tile_expert, N_EXPERTS - 1)
    tile_valid = (tile_start < pad_end[-1]).astype(jnp.int32)
    segs = (seg_off, seg_slot, seg // MOE_CHUNK)
    tail_end = pad_end.at[N_EXPERTS - 1].set(n_tiles * EXPERT_TILE)
    tails = (pad_start + used, (tail_end - pad_start - used) // MOE_CHUNK)
    return segs, tails, tile_expert, tile_valid, n_tiles


def _trunk(groups, meta_tokens, norm_mix_e, w_in, q_gain, k_gain, rel_bias, conv_w, conv_b,
           conv_ln_g, conv_ln_b, w_out, norm_ffn_e, ffn_w1, ffn_w3, ffn_w2,
           norm_mix_o, pool_w, pool_scale, norm_ffn_o, router_w, moe_w1, moe_w3, moe_w2):
    row = lambda a: a.reshape(1, -1).astype(F32)
    shapes = [(g.shape[0], g.shape[1]) for g in groups]
    xs = [g.reshape(-1, D_MODEL) for g in groups]

    hsum = jnp.asarray(np.kron(np.eye(N_HEADS), np.ones((HEAD_DIM, HEAD_DIM))), BF16)
    qg = row(jnp.tile(q_gain[0], N_HEADS))
    kg = row(jnp.tile(k_gain[0], N_HEADS))
    w_in_b = w_in[0].astype(BF16)
    g_mix = row(norm_mix_e[0])
    bias_pat = _attn_bias_patterns(rel_bias[0])
    conv_w_p = jnp.concatenate([conv_w[0].astype(F32), jnp.zeros((1, CONV_DIM), F32)], axis=0)
    w_out_a = w_out[0, :ATTN_DIM].astype(BF16)
    w_out_c = w_out[0, ATTN_DIM:].astype(BF16)
    g_ffn = row(norm_ffn_e[0])
    w1b, w3b, w2b = ffn_w1[0].astype(BF16), ffn_w3[0].astype(BF16), ffn_w2[0].astype(BF16)

    meta = meta_tokens.astype(F32)
    qm, km, vm, cm = _proj(meta, g_mix, w_in_b, qg, kg, hsum, N_META)
    attn_m = _meta_attention(qm, km, vm)

    x2s, x2ms = [], []
    for x, (bsz, seq) in zip(xs, shapes):
        q, k, v, c = _proj(x, g_mix, w_in_b, qg, kg, hsum, TOKEN_TILE)
        attn = _attention(q, k, v, km, vm, bias_pat, bsz, seq)
        cact, cact_m = _conv(c, cm, conv_w_p, row(conv_b[0]), row(conv_ln_g[0]), row(conv_ln_b[0]),
                             bsz, seq, TOKEN_TILE)
        x2s.append(_out_ffn(x, attn, cact, w_out_a, w_out_c, g_ffn, w1b, w3b, w2b, TOKEN_TILE))
        x2ms.append(_out_ffn(jnp.tile(meta, (bsz, 1)), jnp.tile(attn_m, (bsz, 1)), cact_m, w_out_a, w_out_c,
                             g_ffn, w1b, w3b, w2b, bsz * N_META))

    g_pool = row(norm_mix_o[0])
    w_pool_b = pool_w[0].astype(BF16)
    g_moe = row(norm_ffn_o[0])
    rw = jnp.zeros((D_MODEL, 128), F32).at[:, :N_EXPERTS].set(router_w[0].astype(F32))
    r_hi = rw.astype(BF16)
    r_lo = (rw - r_hi.astype(F32)).astype(BF16)
    mw1, mw3, mw2 = moe_w1[0].astype(BF16), moe_w3[0].astype(BF16), moe_w2[0].astype(BF16)

    assert TOKEN_TILE == MOE_TILE
    routed = [_pool_router(x2, x2m, g_pool, w_pool_b, row(pool_scale[0]), g_moe, r_hi, r_lo, bsz, seq, MOE_TILE)
              for x2, x2m, (bsz, seq) in zip(x2s, x2ms, shapes)]
    counts = jnp.concatenate([r[5][:, 0, :N_EXPERTS] for r in routed], axis=0)
    segs, tails, tile_expert, tile_valid, n_tiles = _route(counts)
    pos_t = jnp.concatenate([r[4] for r in routed], axis=1)
    slots = _dispatch(tuple(a.reshape(-1) for a in segs), tails, pos_t, [r[1] for r in routed],
                      [r[2] for r in routed], n_tiles * EXPERT_TILE)
    y = _experts(tile_expert, tile_valid, slots, mw1, mw3, mw2)
    outs, t0 = [], 0
    for (x3, _, _, pos, _, cnt), (bsz, seq) in zip(routed, shapes):
        t1 = t0 + cnt.shape[0]
        seg_tables = tuple(a[t0:t1].reshape(-1) for a in segs)
        outs.append(_combine(seg_tables, pos, x3, y).reshape(bsz, seq, D_MODEL))
        t0 = t1
    return tuple(outs)


def kernel(x_prompt, x_sample, meta_tokens, norm_mix_e, w_in, q_gain, k_gain, rel_bias, conv_w, conv_b, conv_ln_g, conv_ln_b, w_out, norm_ffn_e, ffn_w1, ffn_w3, ffn_w2, norm_mix_o, pool_w, pool_scale, norm_ffn_o, router_w, moe_w1, moe_w3, moe_w2):
    return _trunk([x_prompt, x_sample], meta_tokens, norm_mix_e, w_in, q_gain, k_gain, rel_bias, conv_w,
                  conv_b, conv_ln_g, conv_ln_b, w_out, norm_ffn_e, ffn_w1, ffn_w3, ffn_w2,
                  norm_mix_o, pool_w, pool_scale, norm_ffn_o, router_w, moe_w1, moe_w3, moe_w2)
```

```python
import functools

import numpy as np
import jax
import jax.numpy as jnp
from jax import lax
from jax.experimental import pallas as pl
from jax.experimental.pallas import tpu as pltpu

D_MODEL = 1024
N_META = 16
GRID_W = 64
N_HEADS = 8
HEAD_DIM = 64
ATTN_DIM = N_HEADS * HEAD_DIM
CONV_DIM = D_MODEL - ATTN_DIM
IN_DIM = 3 * ATTN_DIM + 2 * CONV_DIM
WIN_ROWS = 8
WIN_COLS = 16
CONV_WIDTH = 31
CONV_HALF = CONV_WIDTH // 2
POOL_WINDOWS = (2, 4, 8, 16)
POOL_GROUP_DIM = D_MODEL // len(POOL_WINDOWS)
D_FF = 2816
N_EXPERTS = 8
TOP_K = 2
D_FF_EXPERT = 3584
EPS = 1e-6

F32 = jnp.float32
BF16 = jnp.bfloat16
SUBLANES = 8

TOKEN_TILE = 512
ATTN_ROWS = 4
ATTN_Q = ATTN_ROWS * GRID_W
ATTN_KROWS = ATTN_ROWS + WIN_ROWS
ATTN_KBLK = ATTN_KROWS * GRID_W // ATTN_Q
ATTN_GROUP = 4
CONV_HALO = 16
CONV_CHUNK = 64
POOL_HALO = 8
EXPERT_TILE = 1024
EXPERT_FF_CHUNK = 512
MOE_TILE = 512
MOE_CHUNK = 8
MOE_BIG = 32
MOE_ROWS = 1152
MOE_ROW_BLOCK = 384
SLOT_W = D_MODEL + 128
NEG_INF = -1e30
VMEM_LIMIT = 56 * 1024 * 1024


def _params(sem):
    return pltpu.CompilerParams(dimension_semantics=sem, vmem_limit_bytes=VMEM_LIMIT)


def _const_spec(shape):
    zeros = (0,) * len(shape)
    return pl.BlockSpec(shape, lambda *_: zeros)


def _row_spec(tm, cols):
    return pl.BlockSpec((tm, cols), lambda i: (i, 0))


def _rms(x, gain):
    ms = jnp.mean(x * x, axis=-1, keepdims=True)
    return x * lax.rsqrt(ms + EPS) * gain


def _proj_kernel(x_ref, g_ref, w_ref, qg_ref, kg_ref, hsum_ref, q_ref, k_ref, v_ref, c_ref):
    n = _rms(x_ref[...], g_ref[...]).astype(BF16)
    proj = jnp.dot(n, w_ref[...], preferred_element_type=F32)
    q = proj[:, :ATTN_DIM]
    k = proj[:, ATTN_DIM:2 * ATTN_DIM]
    v = proj[:, 2 * ATTN_DIM:3 * ATTN_DIM]
    u = proj[:, 3 * ATTN_DIM:3 * ATTN_DIM + CONV_DIM]
    g = proj[:, 3 * ATTN_DIM + CONV_DIM:]

    def head_norm(a, gain):
        ss = jnp.dot((a * a).astype(BF16), hsum_ref[...], preferred_element_type=F32)
        return a * lax.rsqrt(ss * (1.0 / HEAD_DIM) + EPS) * gain

    q_ref[...] = (head_norm(q, qg_ref[...]) * (HEAD_DIM ** -0.5)).astype(BF16)
    k_ref[...] = head_norm(k, kg_ref[...]).astype(BF16)
    v_ref[...] = v.astype(BF16)
    c_ref[...] = (u * jax.nn.sigmoid(g)).astype(BF16)


def _proj(x, norm_g, w_in, qg, kg, hsum, tm):
    n = x.shape[0]
    out = jax.ShapeDtypeStruct((n, ATTN_DIM), BF16)
    return pl.pallas_call(
        _proj_kernel,
        grid=(n // tm,),
        in_specs=[_row_spec(tm, D_MODEL), _const_spec((1, D_MODEL)), _const_spec((D_MODEL, IN_DIM)),
                  _const_spec((1, ATTN_DIM)), _const_spec((1, ATTN_DIM)), _const_spec((ATTN_DIM, ATTN_DIM))],
        out_specs=[_row_spec(tm, ATTN_DIM)] * 4,
        out_shape=[out] * 4,
        compiler_params=_params(("arbitrary",)),
        name="in_proj",
    )(x, norm_g, w_in, qg, kg, hsum)


def _attn_bias_patterns(rel_bias):
    n_dr, n_dc = 2 * WIN_ROWS - 1, 2 * WIN_COLS - 1
    qr = np.arange(ATTN_ROWS)[:, None]
    kr = np.arange(ATTN_KROWS)[None, :]
    qc = np.arange(GRID_W)[:, None]
    kc = np.arange(GRID_W)[None, :]
    cs = np.clip(qc - WIN_COLS // 2, 0, GRID_W - WIN_COLS)
    col_ok = (kc >= cs) & (kc < cs + WIN_COLS)
    dc = np.clip(kc - qc + (WIN_COLS - 1), 0, n_dc - 1)
    col_sel = (dc[None] == np.arange(n_dc)[:, None, None]).astype(np.float32)
    row_sel, ok = [], []
    for off, rs in ((0, 0 * qr), (ATTN_ROWS, qr), (2 * ATTN_ROWS, ATTN_ROWS + 0 * qr)):
        row_ok = (kr >= rs) & (kr < rs + WIN_ROWS)
        dr = np.clip(kr - (off + qr) + (WIN_ROWS - 1), 0, n_dr - 1)
        row_sel.append((dr[..., None] == np.arange(n_dr)).astype(np.float32))
        ok.append(row_ok[:, None, :, None] & col_ok[None, :, None, :])
    row_sel = np.stack(row_sel)
    ok = np.stack(ok).reshape(3, 1, ATTN_Q, ATTN_KROWS * GRID_W)
    vals = jnp.einsum('pqka,hab,bcd->phqckd', row_sel, rel_bias.astype(F32), col_sel,
                      precision=lax.Precision.HIGHEST)
    vals = vals.reshape(3, N_HEADS, ATTN_Q, ATTN_KROWS * GRID_W)
    return jnp.where(ok, vals, NEG_INF)


def _attn_kernel(q_ref, k0_ref, k1_ref, k2_ref, v0_ref, v1_ref, v2_ref, km_ref, vm_ref, bias_ref, o_ref):
    nt = (((1,), (1,)), ((), ()))
    k_refs = (k0_ref, k1_ref, k2_ref)
    v_refs = (v0_ref, v1_ref, v2_ref)
    outs = []
    for g in range(N_HEADS // ATTN_GROUP):
        group_cols = slice(g * ATTN_GROUP * HEAD_DIM, (g + 1) * ATTN_GROUP * HEAD_DIM)
        p_loc = [[] for _ in range(ATTN_KBLK)]
        p_meta, denom = [], []
        for h in range(g * ATTN_GROUP, (g + 1) * ATTN_GROUP):
            sl = slice(h * HEAD_DIM, (h + 1) * HEAD_DIM)
            qh = q_ref[:, sl]
            s_meta = lax.dot_general(qh, km_ref[:, sl], nt, preferred_element_type=F32)
            s_loc = [lax.dot_general(qh, k_refs[t][:, sl], nt, preferred_element_type=F32)
                     + bias_ref[0, h, :, t * ATTN_Q:(t + 1) * ATTN_Q] for t in range(ATTN_KBLK)]
            m_loc = functools.reduce(jnp.maximum, s_loc)
            m = jnp.maximum(jnp.max(s_meta, axis=-1, keepdims=True), jnp.max(m_loc, axis=-1, keepdims=True))
            pm = jnp.exp(s_meta - m)
            p_sum = None
            for t in range(ATTN_KBLK):
                p = jnp.exp(s_loc[t] - m)
                p_sum = p if p_sum is None else p_sum + p
                p_loc[t].append(p.astype(BF16))
            p_meta.append(pm.astype(BF16))
            denom.append(jnp.sum(pm, axis=-1, keepdims=True) + jnp.sum(p_sum, axis=-1, keepdims=True))
        acc = jnp.dot(jnp.concatenate(p_meta, axis=0), vm_ref[:, group_cols], preferred_element_type=F32)
        for t in range(ATTN_KBLK):
            acc = acc + jnp.dot(jnp.concatenate(p_loc[t], axis=0), v_refs[t][:, group_cols],
                                preferred_element_type=F32)
        for i in range(ATTN_GROUP):
            rows = slice(i * ATTN_Q, (i + 1) * ATTN_Q)
            outs.append(acc[rows, i * HEAD_DIM:(i + 1) * HEAD_DIM] / denom[i])
    o_ref[...] = jnp.concatenate(outs, axis=-1).astype(BF16)


def _attention(q, k, v, km, vm, bias_pat, bsz, seq):
    nb = seq // ATTN_Q
    assert seq % ATTN_Q == 0 and nb >= ATTN_KBLK

    def q_map(b, j):
        return (b * nb + j, 0)

    def kv_map(t):
        return lambda b, j: (b * nb + jnp.clip(j - 1, 0, nb - ATTN_KBLK) + t, 0)

    def bias_map(b, j):
        return (jnp.where(j == 0, 0, jnp.where(j == nb - 1, 2, 1)), 0, 0, 0)

    blk = (ATTN_Q, ATTN_DIM)
    kv_specs = [pl.BlockSpec(blk, kv_map(t)) for t in range(ATTN_KBLK)]
    return pl.pallas_call(
        _attn_kernel,
        grid=(bsz, nb),
        in_specs=[pl.BlockSpec(blk, q_map)] + kv_specs + kv_specs
                 + [_const_spec((N_META, ATTN_DIM)), _const_spec((N_META, ATTN_DIM)),
                    pl.BlockSpec((1, N_HEADS, ATTN_Q, ATTN_KBLK * ATTN_Q), bias_map)],
        out_specs=pl.BlockSpec(blk, q_map),
        out_shape=jax.ShapeDtypeStruct((bsz * seq, ATTN_DIM), BF16),
        compiler_params=_params(("arbitrary", "arbitrary")),
        name="nbr_attn",
    )(q, k, k, k, v, v, v, km, vm, bias_pat)


def _meta_attn_kernel(q_ref, k_ref, v_ref, o_ref):
    nt = (((1,), (1,)), ((), ()))
    outs = []
    for h in range(N_HEADS):
        sl = slice(h * HEAD_DIM, (h + 1) * HEAD_DIM)
        s = lax.dot_general(q_ref[:, sl], k_ref[:, sl], nt, preferred_element_type=F32)
        p = jnp.exp(s - jnp.max(s, axis=-1, keepdims=True))
        o = jnp.dot(p.astype(BF16), v_ref[:, sl], preferred_element_type=F32)
        outs.append(o / jnp.sum(p, axis=-1, keepdims=True))
    o_ref[...] = jnp.concatenate(outs, axis=-1).astype(BF16)


def _meta_attention(qm, km, vm):
    spec = _const_spec((N_META, ATTN_DIM))
    return pl.pallas_call(
        _meta_attn_kernel,
        in_specs=[spec] * 3, out_specs=spec, grid=(1,),
        out_shape=jax.ShapeDtypeStruct((N_META, ATTN_DIM), BF16),
        compiler_params=_params(("arbitrary",)),
        name="meta_attn",
    )(qm, km, vm)


def _conv_kernel(c_ref, prev_ref, next_ref, cm_ref, w_ref, b_ref, lg_ref, lb_ref,
                 out_ref, outm_ref, xs_ref, mpad_ref):
    tile = c_ref.shape[0]
    n_pad = tile + 2 * CONV_HALO
    i = pl.program_id(1)
    first = i == 0
    last = i == pl.num_programs(1) - 1
    cm = cm_ref[...].astype(F32)
    xs_ref[0, 0:CONV_HALO, :] = jnp.where(first, cm, prev_ref[...].astype(F32))
    xs_ref[0, CONV_HALO:CONV_HALO + tile, :] = c_ref[...].astype(F32)
    xs_ref[0, CONV_HALO + tile:, :] = jnp.where(last, 0.0, next_ref[...].astype(F32))
    for s in range(1, SUBLANES):
        xs_ref[s, 0:n_pad - SUBLANES, :] = xs_ref[0, s:s + n_pad - SUBLANES, :]

    def conv_rows(read, rows):
        acc = jnp.zeros((rows, CONV_DIM), F32)
        for j in range(CONV_WIDTH):
            acc = acc + w_ref[j:j + 1, :] * read(j - CONV_HALF)
        y = acc + b_ref[...]
        mu = jnp.mean(y, axis=-1, keepdims=True)
        yc = y - mu
        var = jnp.mean(yc * yc, axis=-1, keepdims=True)
        z = yc * lax.rsqrt(var + EPS) * lg_ref[...] + lb_ref[...]
        return (z * jax.nn.sigmoid(z)).astype(BF16)

    for rc in range(tile // CONV_CHUNK):
        start = CONV_HALO + rc * CONV_CHUNK

        def read(d, start=start):
            lo = start + d
            return xs_ref[lo % SUBLANES, lo - lo % SUBLANES:lo - lo % SUBLANES + CONV_CHUNK, :]

        out_ref[rc * CONV_CHUNK:(rc + 1) * CONV_CHUNK, :] = conv_rows(read, CONV_CHUNK)

    @pl.when(first)
    def _():
        mpad_ref[0:N_META, :] = jnp.zeros((N_META, CONV_DIM), F32)
        mpad_ref[N_META:2 * N_META, :] = cm
        mpad_ref[2 * N_META:, :] = c_ref[0:N_META, :].astype(F32)
        outm_ref[...] = conv_rows(lambda d: mpad_ref[N_META + d:2 * N_META + d, :], N_META)


def _conv(c, cm, conv_w, conv_b, ln_g, ln_b, bsz, seq, tile):
    tps = seq // tile
    hpt = tile // CONV_HALO
    n_halo = bsz * seq // CONV_HALO

    def main_map(b, i):
        return (b * tps + i, 0)

    def prev_map(b, i):
        return (jnp.maximum((b * tps + i) * hpt - 1, 0), 0)

    def next_map(b, i):
        return (jnp.minimum((b * tps + i + 1) * hpt, n_halo - 1), 0)

    halo = (CONV_HALO, CONV_DIM)
    vec = _const_spec((1, CONV_DIM))
    return pl.pallas_call(
        _conv_kernel,
        grid=(bsz, tps),
        in_specs=[pl.BlockSpec((tile, CONV_DIM), main_map), pl.BlockSpec(halo, prev_map),
                  pl.BlockSpec(halo, next_map), _const_spec((N_META, CONV_DIM)),
                  _const_spec((CONV_WIDTH + 1, CONV_DIM)), vec, vec, vec],
        out_specs=[pl.BlockSpec((tile, CONV_DIM), main_map),
                   pl.BlockSpec((N_META, CONV_DIM), lambda b, i: (b, 0))],
        out_shape=[jax.ShapeDtypeStruct((bsz * seq, CONV_DIM), BF16),
                   jax.ShapeDtypeStruct((bsz * N_META, CONV_DIM), BF16)],
        scratch_shapes=[pltpu.VMEM((SUBLANES, tile + 2 * CONV_HALO, CONV_DIM), F32),
                        pltpu.VMEM((3 * N_META, CONV_DIM), F32)],
        compiler_params=_params(("arbitrary", "arbitrary")),
        name="conv_ln_silu",
    )(c, c, c, cm, conv_w, conv_b, ln_g, ln_b)


def _out_ffn_kernel(x_ref, a_ref, c_ref, wa_ref, wc_ref, g_ref, w1_ref, w3_ref, w2_ref, o_ref):
    x = (x_ref[...]
         + jnp.dot(a_ref[...], wa_ref[...], preferred_element_type=F32)
         + jnp.dot(c_ref[...], wc_ref[...], preferred_element_type=F32))
    n = _rms(x, g_ref[...]).astype(BF16)
    h1 = jnp.dot(n, w1_ref[...], preferred_element_type=F32)
    h3 = jnp.dot(n, w3_ref[...], preferred_element_type=F32)
    act = (h1 * jax.nn.sigmoid(h1) * h3).astype(BF16)
    o_ref[...] = x + jnp.dot(act, w2_ref[...], preferred_element_type=F32)


def _out_ffn(x, attn, cact, w_attn, w_conv, g, w1, w3, w2, tm):
    n = x.shape[0]
    wspec = _const_spec((ATTN_DIM, D_MODEL))
    return pl.pallas_call(
        _out_ffn_kernel,
        grid=(n // tm,),
        in_specs=[_row_spec(tm, D_MODEL), _row_spec(tm, ATTN_DIM), _row_spec(tm, CONV_DIM), wspec, wspec,
                  _const_spec((1, D_MODEL)), _const_spec((D_MODEL, D_FF)), _const_spec((D_MODEL, D_FF)),
                  _const_spec((D_FF, D_MODEL))],
        out_specs=_row_spec(tm, D_MODEL),
        out_shape=jax.ShapeDtypeStruct((n, D_MODEL), F32),
        compiler_params=_params(("arbitrary",)),
        name="out_proj_swiglu",
    )(x, attn, cact, w_attn, w_conv, g, w1, w3, w2)


def _pool_router_kernel(x_ref, prev_ref, next_ref, xm_ref, g_ref, wp_ref, ps_ref, g2_ref, rhi_ref, rlo_ref,
                        x3_ref, hn_ref, gate_ref, pos_ref, pos_t_ref, cnt_ref, npad_ref, *sum_refs, seq):
    tile = x_ref.shape[0]
    i = pl.program_id(1)
    first = i == 0
    last = i == pl.num_programs(1) - 1
    gain = g_ref[...]
    x = x_ref[...]
    n_main = _rms(x, gain)
    npad_ref[0:POOL_HALO, :] = _rms(jnp.where(first, xm_ref[...], prev_ref[...]), gain)
    npad_ref[POOL_HALO:POOL_HALO + tile, :] = n_main
    npad_ref[POOL_HALO + tile:, :] = jnp.where(last, 0.0, _rms(next_ref[...], gain))

    def window_sum(cols, w):
        first = POOL_HALO - w // 2
        needs, need, m = [], first + tile, w // 2
        while m >= 1:
            need += m
            needs.append((m, need))
            m //= 2
        needs.reverse()
        read = lambda lo, n: npad_ref[lo:lo + n, cols]
        for level, ((m, _), (_, n_next)) in enumerate(zip(needs, needs[1:])):
            buf = sum_refs[level % 2]
            buf[0:n_next, :] = read(0, n_next) + read(m, n_next)
            read = lambda lo, n, buf=buf: buf[lo:lo + n, :]
        return read(first, tile) + read(first + w // 2, tile)

    tok = i * tile + lax.broadcasted_iota(jnp.int32, (tile, 1), 0)
    mixed = []
    for gi, w in enumerate(POOL_WINDOWS):
        cols = slice(gi * POOL_GROUP_DIM, (gi + 1) * POOL_GROUP_DIM)
        half = w // 2
        acc = window_sum(cols, w)
        count = (w - jnp.maximum(tok + half - seq, 0)).astype(F32)
        diff = (acc / count - n_main[:, cols]).astype(BF16)
        mixed.append(jnp.dot(diff, wp_ref[gi], preferred_element_type=F32))
    x3 = x + jnp.concatenate(mixed, axis=-1) * ps_ref[...]
    x3_ref[...] = x3

    hn = _rms(x3, g2_ref[...])
    hn_ref[...] = hn
    h_hi = hn.astype(BF16)
    h_lo = (hn - h_hi.astype(F32)).astype(BF16)
    def logits_of(rows):
        return (jnp.dot(h_hi[rows], rhi_ref[...], preferred_element_type=F32)
                + jnp.dot(h_lo[rows], rhi_ref[...], preferred_element_type=F32)
                + jnp.dot(h_hi[rows], rlo_ref[...], preferred_element_type=F32))

    logits = jnp.concatenate([logits_of(slice(0, tile // 2)), logits_of(slice(tile // 2, tile))],
                             axis=0)
    lane = lax.broadcasted_iota(jnp.int32, logits.shape, 1)
    lane_f = lane.astype(F32)
    logits = jnp.where(lane < N_EXPERTS, logits, NEG_INF)
    m1 = jnp.max(logits, axis=-1, keepdims=True)
    i1 = jnp.min(jnp.where(logits == m1, lane_f, 256.0), axis=-1, keepdims=True)
    rest = jnp.where(lane_f == i1, NEG_INF, logits)
    m2 = jnp.max(rest, axis=-1, keepdims=True)
    i2 = jnp.min(jnp.where(rest == m2, lane_f, 256.0), axis=-1, keepdims=True)
    e2 = jnp.exp(m2 - m1)
    g1 = 1.0 / (1.0 + e2)
    gate_ref[...] = jnp.where(lane == 0, g1, jnp.where(lane == 1, e2 * g1, 0.0))

    hot = [jnp.where(lane_f == i1, 1.0, 0.0), jnp.where(lane_f == i2, 1.0, 0.0)]
    earlier = jnp.where(lax.broadcasted_iota(jnp.int32, (tile, tile), 1)
                        < lax.broadcasted_iota(jnp.int32, (tile, tile), 0), 1.0, 0.0).astype(BF16)
    before = [jnp.dot(earlier, h.astype(BF16), preferred_element_type=F32) for h in hot]
    count = [jnp.sum(h, axis=0, keepdims=True) for h in hot]
    total = count[0] + count[1]
    chunks = jnp.floor((total + (MOE_CHUNK - 1)) * (1.0 / MOE_CHUNK))
    lower_expert = jnp.where(lax.broadcasted_iota(jnp.int32, (128, 128), 0)
                             < lax.broadcasted_iota(jnp.int32, (128, 128), 1), 1.0, 0.0).astype(BF16)
    seg_off = MOE_CHUNK * jnp.dot(jnp.broadcast_to(chunks, (SUBLANES, 128)).astype(BF16), lower_expert,
                                  preferred_element_type=F32)[0:1, :]
    pos0 = jnp.sum(hot[0] * (before[0] + seg_off), axis=-1, keepdims=True)
    pos1 = jnp.sum(hot[1] * (before[1] + count[0] + seg_off), axis=-1, keepdims=True)
    pos = jnp.where(lane == 0, pos0, jnp.where(lane == 1, pos1, 0.0))
    pos_ref[...] = pos.astype(jnp.int32)
    pos_t_ref[...] = jnp.transpose(pos)[0:SUBLANES, :].astype(jnp.int32)
    cnt_ref[...] = jnp.broadcast_to(total, (SUBLANES, 128)).astype(jnp.int32)


def _pool_router(x2, x2m, g, w_pool, pool_scale, g2, r_hi, r_lo, bsz, seq, tile):
    tps = seq // tile
    hpt = tile // POOL_HALO
    n_halo = bsz * seq // POOL_HALO

    def main_map(b, i):
        return (b * tps + i, 0)

    def prev_map(b, i):
        return (jnp.maximum((b * tps + i) * hpt - 1, 0), 0)

    def next_map(b, i):
        return (jnp.minimum((b * tps + i + 1) * hpt, n_halo - 1), 0)

    halo = (POOL_HALO, D_MODEL)
    vec = _const_spec((1, D_MODEL))
    main = pl.BlockSpec((tile, D_MODEL), main_map)
    small = pl.BlockSpec((tile, 128), main_map)
    n = bsz * seq
    return pl.pallas_call(
        functools.partial(_pool_router_kernel, seq=seq),
        grid=(bsz, tps),
        in_specs=[main, pl.BlockSpec(halo, prev_map), pl.BlockSpec(halo, next_map),
                  pl.BlockSpec(halo, lambda b, i: (2 * b + 1, 0)), vec,
                  _const_spec((len(POOL_WINDOWS), POOL_GROUP_DIM, POOL_GROUP_DIM)), vec, vec,
                  _const_spec((D_MODEL, 128)), _const_spec((D_MODEL, 128))],
        out_specs=[main, main, small, small,
                   pl.BlockSpec((SUBLANES, tile), lambda b, i: (0, b * tps + i)),
                   pl.BlockSpec((None, SUBLANES, 128), lambda b, i: (b * tps + i, 0, 0))],
        out_shape=[jax.ShapeDtypeStruct((n, D_MODEL), F32), jax.ShapeDtypeStruct((n, D_MODEL), F32),
                   jax.ShapeDtypeStruct((n, 128), F32), jax.ShapeDtypeStruct((n, 128), jnp.int32),
                   jax.ShapeDtypeStruct((SUBLANES, n), jnp.int32),
                   jax.ShapeDtypeStruct((n // tile, SUBLANES, 128), jnp.int32)],
        scratch_shapes=[pltpu.VMEM((tile + 2 * POOL_HALO, D_MODEL), F32)]
                       + [pltpu.VMEM((tile + 2 * POOL_HALO, POOL_GROUP_DIM), F32)] * 2,
        compiler_params=_params(("arbitrary", "arbitrary")),
        name="pool_router",
    )(x2, x2, x2, x2m, g, w_pool, pool_scale, g2, r_hi, r_lo)


def _split_chunks(n_chunks):
    per_big = MOE_BIG // MOE_CHUNK
    n_big = lax.div(n_chunks, per_big)
    return n_big, n_chunks - n_big * per_big


def _start_pieces(copy, first_a, first_b, n_chunks, priority=0):
    n_big, n_small = _split_chunks(n_chunks)

    def start(rows, base_a, base_b):
        def body(c, carry):
            copy(pl.multiple_of(base_a + c * rows, MOE_CHUNK), pl.multiple_of(base_b + c * rows, MOE_CHUNK),
                 rows).start(priority=priority)
            return carry
        return body

    lax.fori_loop(0, n_big, start(MOE_BIG, first_a, first_b), 0)
    lax.fori_loop(0, n_small, start(MOE_CHUNK, first_a + n_big * MOE_BIG, first_b + n_big * MOE_BIG), 0)


def _wait_pieces(copy, chunk_counts):
    splits = [_split_chunks(n) for n in chunk_counts]
    for rows, count in ((MOE_BIG, sum(s[0] for s in splits)), (MOE_CHUNK, sum(s[1] for s in splits))):
        def wait(c, carry, rows=rows):
            copy(0, 0, rows).wait()
            return carry

        lax.fori_loop(0, count, wait, 0)


def _start_segments(copy, off_ref, slot_ref, nch_ref, t):
    for e in range(N_EXPERTS):
        _start_pieces(copy, off_ref[t * N_EXPERTS + e], slot_ref[t * N_EXPERTS + e], nch_ref[t * N_EXPERTS + e],
                      priority=e % 2)


def _wait_segments(copy, nch_ref, t):
    _wait_pieces(copy, [nch_ref[t * N_EXPERTS + e] for e in range(N_EXPERTS)])


def _dispatch_kernel(off_ref, slot_ref, nch_ref, tail_ref, ntail_ref, pos_ref, *refs, group_tiles):
    n_groups = len(group_tiles) - 1
    hn_refs, gate_refs = refs[:n_groups], refs[n_groups:2 * n_groups]
    xs_ref, buf_ref, zero_ref, sem = refs[2 * n_groups:]
    t = pl.program_id(0)
    cur = lax.rem(t, 2)

    def copy_from(b):
        def copy(buf_row, slot_row, rows):
            return pltpu.make_async_copy(buf_ref.at[b, pl.ds(buf_row, rows)], xs_ref.at[pl.ds(slot_row, rows)],
                                         sem.at[b])
        return copy

    @pl.when(t >= 2)
    def _():
        _wait_segments(copy_from(cur), nch_ref, t - 2)

    for g in range(n_groups):
        @pl.when((t >= group_tiles[g]) & (t < group_tiles[g + 1]))
        def _(g=g):
            hn = hn_refs[g][...].astype(BF16)
            gates = gate_refs[g][...]
            lane = lax.broadcasted_iota(jnp.int32, gates.shape, 1)
            pieces = []
            for k in range(TOP_K):
                gk = gates[:, k:k + 1]
                hi = gk.astype(BF16).astype(F32)
                mid = (gk - hi).astype(BF16).astype(F32)
                pieces.append(jnp.where(lane == 0, hi, jnp.where(lane == 1, mid, jnp.where(lane == 2, gk - hi - mid, 0.0)))
                              .astype(BF16))
            for r0 in range(0, MOE_ROWS, MOE_ROW_BLOCK):
                rows = r0 + lax.broadcasted_iota(jnp.int32, (MOE_ROW_BLOCK, MOE_TILE), 0)
                own = [jnp.where(rows == pos_ref[k:k + 1, :], 1.0, 0.0).astype(BF16) for k in range(TOP_K)]
                block = slice(r0, r0 + MOE_ROW_BLOCK)
                buf_ref[cur, block, 0:D_MODEL] = jnp.dot(own[0] + own[1], hn, preferred_element_type=F32)
                buf_ref[cur, block, D_MODEL:] = sum(jnp.dot(own[k], pieces[k], preferred_element_type=F32)
                                                    for k in range(TOP_K))

    _start_segments(copy_from(cur), off_ref, slot_ref, nch_ref, t)

    @pl.when(t == pl.num_programs(0) - 1)
    def _():
        @pl.when(t >= 1)
        def _():
            _wait_segments(copy_from(1 - cur), nch_ref, t - 1)

        _wait_segments(copy_from(cur), nch_ref, t)

        zero_ref[...] = jnp.zeros(zero_ref.shape, F32)

        def fill(unused_row, slot_row, rows):
            del unused_row
            return pltpu.make_async_copy(zero_ref.at[pl.ds(0, rows)], xs_ref.at[pl.ds(slot_row, rows)], sem.at[0])

        for e in range(N_EXPERTS):
            _start_pieces(fill, 0, tail_ref[e], ntail_ref[e])
        _wait_pieces(fill, [ntail_ref[e] for e in range(N_EXPERTS)])


def _dispatch(seg_tables, tail_tables, pos_t, hns, gates, n_slots):
    group_tiles = [0]
    for hn in hns:
        group_tiles.append(group_tiles[-1] + hn.shape[0] // MOE_TILE)

    def group_spec(g, cols):
        first, count = group_tiles[g], group_tiles[g + 1] - group_tiles[g]
        return pl.BlockSpec((MOE_TILE, cols), lambda i, *_: (jnp.clip(i - first, 0, count - 1), 0))

    groups = range(len(hns))
    grid_spec = pltpu.PrefetchScalarGridSpec(
        num_scalar_prefetch=5,
        grid=(group_tiles[-1],),
        in_specs=[pl.BlockSpec((SUBLANES, MOE_TILE), lambda i, *_: (0, i))]
                 + [group_spec(g, D_MODEL) for g in groups] + [group_spec(g, 128) for g in groups],
        out_specs=pl.BlockSpec(memory_space=pl.ANY),
        scratch_shapes=[pltpu.VMEM((2, MOE_ROWS, SLOT_W), F32), pltpu.VMEM((MOE_BIG, SLOT_W), F32),
                        pltpu.SemaphoreType.DMA((2,))],
    )
    return pl.pallas_call(
        functools.partial(_dispatch_kernel, group_tiles=tuple(group_tiles)),
        grid_spec=grid_spec,
        out_shape=jax.ShapeDtypeStruct((n_slots, SLOT_W), F32),
        compiler_params=_params(("arbitrary",)),
        name="moe_dispatch",
    )(*seg_tables, *tail_tables, pos_t, *hns, *gates)


def _expert_kernel(te_ref, tv_ref, x_ref, w1_ref, w3_ref, w2_ref, y_ref, act_ref):
    del te_ref
    valid = tv_ref[pl.program_id(0)] == 1
    n_chunks = act_ref.shape[0]

    @pl.when(valid)
    def _():
        xb = x_ref[:, 0:D_MODEL].astype(BF16)
        gate = jnp.sum(x_ref[:, D_MODEL:], axis=-1, keepdims=True)
        for c in range(n_chunks):
            cols = slice(c * EXPERT_FF_CHUNK, (c + 1) * EXPERT_FF_CHUNK)
            h1 = jnp.dot(xb, w1_ref[:, cols], preferred_element_type=F32)
            h3 = jnp.dot(xb, w3_ref[:, cols], preferred_element_type=F32)
            act_ref[c] = (h1 * jax.nn.sigmoid(h1) * h3).astype(BF16)
        acc = jnp.dot(act_ref[0], w2_ref[0:EXPERT_FF_CHUNK, :], preferred_element_type=F32)
        for c in range(1, n_chunks):
            acc = acc + jnp.dot(act_ref[c], w2_ref[c * EXPERT_FF_CHUNK:(c + 1) * EXPERT_FF_CHUNK, :],
                                preferred_element_type=F32)
        y_ref[...] = acc * gate

    @pl.when(jnp.logical_not(valid))
    def _():
        y_ref[...] = jnp.zeros(y_ref.shape, F32)


def _experts(tile_expert, tile_valid, xs, w1, w3, w2):
    n_slots = xs.shape[0]
    per_expert = dict(pipeline_mode=pl.Buffered(1))
    grid_spec = pltpu.PrefetchScalarGridSpec(
        num_scalar_prefetch=2,
        grid=(n_slots // EXPERT_TILE,),
        in_specs=[pl.BlockSpec((EXPERT_TILE, SLOT_W), lambda j, te, tv: (jnp.where(tv[j] == 1, j, 0), 0)),
                  pl.BlockSpec((None, D_MODEL, D_FF_EXPERT), lambda j, te, tv: (te[j], 0, 0), **per_expert),
                  pl.BlockSpec((None, D_MODEL, D_FF_EXPERT), lambda j, te, tv: (te[j], 0, 0), **per_expert),
                  pl.BlockSpec((None, D_FF_EXPERT, D_MODEL), lambda j, te, tv: (te[j], 0, 0), **per_expert)],
        out_specs=pl.BlockSpec((EXPERT_TILE, D_MODEL), lambda j, te, tv: (j, 0)),
        scratch_shapes=[pltpu.VMEM((D_FF_EXPERT // EXPERT_FF_CHUNK, EXPERT_TILE, EXPERT_FF_CHUNK), BF16)],
    )
    return pl.pallas_call(
        _expert_kernel,
        grid_spec=grid_spec,
        out_shape=jax.ShapeDtypeStruct((n_slots, D_MODEL), F32),
        compiler_params=_params(("arbitrary",)),
        name="moe_experts",
    )(tile_expert, tile_valid, xs, w1, w3, w2)


def _combine_kernel(off_ref, slot_ref, nch_ref, pos_ref, x_ref, y_ref, o_ref, buf_ref, sem):
    t = pl.program_id(0)
    cur = lax.rem(t, 2)

    def copy_into(b):
        def copy(buf_row, slot_row, rows):
            return pltpu.make_async_copy(y_ref.at[pl.ds(slot_row, rows)], buf_ref.at[b, pl.ds(buf_row, rows)],
                                         sem.at[b])
        return copy

    @pl.when(t == 0)
    def _():
        buf_ref[...] = jnp.zeros(buf_ref.shape, F32)
        _start_segments(copy_into(0), off_ref, slot_ref, nch_ref, 0)

    @pl.when(t + 1 < pl.num_programs(0))
    def _():
        _start_segments(copy_into(1 - cur), off_ref, slot_ref, nch_ref, t + 1)

    _wait_segments(copy_into(cur), nch_ref, t)
    yb = buf_ref[cur].astype(BF16)
    cols = lax.broadcasted_iota(jnp.int32, (MOE_TILE, MOE_ROWS), 1)
    pos = pos_ref[...]
    pick = jnp.where(cols == pos[:, 0:1], 1.0, jnp.where(cols == pos[:, 1:2], 1.0, 0.0)).astype(BF16)
    o_ref[...] = x_ref[...] + jnp.dot(pick, yb, preferred_element_type=F32)


def _combine(seg_tables, pos, x3, y):
    n = x3.shape[0]
    grid_spec = pltpu.PrefetchScalarGridSpec(
        num_scalar_prefetch=3,
        grid=(n // MOE_TILE,),
        in_specs=[pl.BlockSpec((MOE_TILE, 128), lambda i, *_: (i, 0)),
                  pl.BlockSpec((MOE_TILE, D_MODEL), lambda i, *_: (i, 0)),
                  pl.BlockSpec(memory_space=pl.ANY)],
        out_specs=pl.BlockSpec((MOE_TILE, D_MODEL), lambda i, *_: (i, 0)),
        scratch_shapes=[pltpu.VMEM((2, MOE_ROWS, D_MODEL), F32), pltpu.SemaphoreType.DMA((2,))],
    )
    return pl.pallas_call(
        _combine_kernel,
        grid_spec=grid_spec,
        out_shape=jax.ShapeDtypeStruct((n, D_MODEL), F32),
        compiler_params=_params(("arbitrary",)),
        name="moe_combine",
    )(*seg_tables, pos, x3, y)


def _route(counts):
    n_tok_tiles = counts.shape[0]
    seg = (counts + MOE_CHUNK - 1) // MOE_CHUNK * MOE_CHUNK
    seg_off = jnp.cumsum(seg, axis=1) - seg
    used = jnp.sum(seg, axis=0)
    padded = (used + EXPERT_TILE - 1) // EXPERT_TILE * EXPERT_TILE
    pad_end = jnp.cumsum(padded)
    pad_start = pad_end - padded
    seg_slot = pad_start[None, :] + jnp.cumsum(seg, axis=0) - seg
    worst_rows = n_tok_tiles * (TOP_K * MOE_TILE + N_EXPERTS * (MOE_CHUNK - 1)) + N_EXPERTS * (EXPERT_TILE - 1)
    n_tiles = (worst_rows + EXPERT_TILE - 1) // EXPERT_TILE
    tile_start = jnp.arange(n_tiles, dtype=jnp.int32) * EXPERT_TILE
    tile_expert = jnp.sum((tile_start[:, None] >= pad_end[None, :]).astype(jnp.int32), axis=1)
    tile_expert = jnp.minimum(tile_expert, N_EXPERTS - 1)
    tile_valid = (tile_start < pad_end[-1]).astype(jnp.int32)
    segs = (seg_off, seg_slot, seg // MOE_CHUNK)
    tail_end = pad_end.at[N_EXPERTS - 1].set(n_tiles * EXPERT_TILE)
    tails = (pad_start + used, (tail_end - pad_start - used) // MOE_CHUNK)
    return segs, tails, tile_expert, tile_valid, n_tiles


def _trunk(groups, meta_tokens, norm_mix_e, w_in, q_gain, k_gain, rel_bias, conv_w, conv_b,
           conv_ln_g, conv_ln_b, w_out, norm_ffn_e, ffn_w1, ffn_w3, ffn_w2,
           norm_mix_o, pool_w, pool_scale, norm_ffn_o, router_w, moe_w1, moe_w3, moe_w2):
    row = lambda a: a.reshape(1, -1).astype(F32)
    shapes = [(g.shape[0], g.shape[1]) for g in groups]
    xs = [g.reshape(-1, D_MODEL) for g in groups]

    hsum = jnp.asarray(np.kron(np.eye(N_HEADS), np.ones((HEAD_DIM, HEAD_DIM))), BF16)
    qg = row(jnp.tile(q_gain[0], N_HEADS))
    kg = row(jnp.tile(k_gain[0], N_HEADS))
    w_in_b = w_in[0].astype(BF16)
    g_mix = row(norm_mix_e[0])
    bias_pat = _attn_bias_patterns(rel_bias[0])
    conv_w_p = jnp.concatenate([conv_w[0].astype(F32), jnp.zeros((1, CONV_DIM), F32)], axis=0)
    w_out_a = w_out[0, :ATTN_DIM].astype(BF16)
    w_out_c = w_out[0, ATTN_DIM:].astype(BF16)
    g_ffn = row(norm_ffn_e[0])
    w1b, w3b, w2b = ffn_w1[0].astype(BF16), ffn_w3[0].astype(BF16), ffn_w2[0].astype(BF16)

    meta = meta_tokens.astype(F32)
    qm, km, vm, cm = _proj(meta, g_mix, w_in_b, qg, kg, hsum, N_META)
    attn_m = _meta_attention(qm, km, vm)

    x2s, x2ms = [], []
    for x, (bsz, seq) in zip(xs, shapes):
        q, k, v, c = _proj(x, g_mix, w_in_b, qg, kg, hsum, TOKEN_TILE)
        attn = _attention(q, k, v, km, vm, bias_pat, bsz, seq)
        cact, cact_m = _conv(c, cm, conv_w_p, row(conv_b[0]), row(conv_ln_g[0]), row(conv_ln_b[0]),
                             bsz, seq, TOKEN_TILE)
        x2s.append(_out_ffn(x, attn, cact, w_out_a, w_out_c, g_ffn, w1b, w3b, w2b, TOKEN_TILE))
        x2ms.append(_out_ffn(jnp.tile(meta, (bsz, 1)), jnp.tile(attn_m, (bsz, 1)), cact_m, w_out_a, w_out_c,
                             g_ffn, w1b, w3b, w2b, bsz * N_META))

    g_pool = row(norm_mix_o[0])
    w_pool_b = pool_w[0].astype(BF16)
    g_moe = row(norm_ffn_o[0])
    rw = jnp.zeros((D_MODEL, 128), F32).at[:, :N_EXPERTS].set(router_w[0].astype(F32))
    r_hi = rw.astype(BF16)
    r_lo = (rw - r_hi.astype(F32)).astype(BF16)
    mw1, mw3, mw2 = moe_w1[0].astype(BF16), moe_w3[0].astype(BF16), moe_w2[0].astype(BF16)

    assert TOKEN_TILE == MOE_TILE
    routed = [_pool_router(x2, x2m, g_pool, w_pool_b, row(pool_scale[0]), g_moe, r_hi, r_lo, bsz, seq, MOE_TILE)
              for x2, x2m, (bsz, seq) in zip(x2s, x2ms, shapes)]
    counts = jnp.concatenate([r[5][:, 0, :N_EXPERTS] for r in routed], axis=0)
    segs, tails, tile_expert, tile_valid, n_tiles = _route(counts)
    pos_t = jnp.concatenate([r[4] for r in routed], axis=1)
    slots = _dispatch(tuple(a.reshape(-1) for a in segs), tails, pos_t, [r[1] for r in routed],
                      [r[2] for r in routed], n_tiles * EXPERT_TILE)
    y = _experts(tile_expert, tile_valid, slots, mw1, mw3, mw2)
    outs, t0 = [], 0
    for (x3, _, _, pos, _, cnt), (bsz, seq) in zip(routed, shapes):
        t1 = t0 + cnt.shape[0]
        seg_tables = tuple(a[t0:t1].reshape(-1) for a in segs)
        outs.append(_combine(seg_tables, pos, x3, y).reshape(bsz, seq, D_MODEL))
        t0 = t1
    return tuple(outs)


def kernel(x_prompt, x_sample, meta_tokens, norm_mix_e, w_in, q_gain, k_gain, rel_bias, conv_w, conv_b, conv_ln_g, conv_ln_b, w_out, norm_ffn_e, ffn_w1, ffn_w3, ffn_w2, norm_mix_o, pool_w, pool_scale, norm_ffn_o, router_w, moe_w1, moe_w3, moe_w2):
    return _trunk([x_prompt, x_sample], meta_tokens, norm_mix_e, w_in, q_gain, k_gain, rel_bias, conv_w,
                  conv_b, conv_ln_g, conv_ln_b, w_out, norm_ffn_e, ffn_w1, ffn_w3, ffn_w2,
                  norm_mix_o, pool_w, pool_scale, norm_ffn_o, router_w, moe_w1, moe_w3, moe_w2)
```
